```python
import jax, jax.numpy as jnp
from jax import lax
import numpy as np

D_MODEL = 1024
BATCH = 32
SEQ = 2048
DEPTH = 1

GRID_W = 64
NA_HEADS = 8
NA_HEAD_DIM = 64
NA_WIDTH = NA_HEADS * NA_HEAD_DIM
NA_KH_MAX = 8
NA_KW = 16
SG_GROUPS = 8
SG_GROUP_DIM = 64
SG_WIDTH = SG_GROUPS * SG_GROUP_DIM
SG_CHUNK = 128
MIX_WIDTH = NA_WIDTH + SG_WIDTH
IN_COLS = 3 * NA_WIDTH + 2 * SG_WIDTH
D_FF = 4 * D_MODEL
EPS = 1e-6

kernel_name = "hybrid_natten_sgu_encoder_block"


def rmsnorm(x, g):
    xf = x.astype(jnp.float32)
    y = xf * lax.rsqrt(jnp.mean(xf * xf, axis=-1, keepdims=True) + EPS) * g.astype(jnp.float32)
    return y.astype(x.dtype)


def layernorm(x, g, b):
    xf = x.astype(jnp.float32)
    mu = jnp.mean(xf, axis=-1, keepdims=True)
    xc = xf - mu
    y = xc * lax.rsqrt(jnp.mean(xc * xc, axis=-1, keepdims=True) + EPS)
    return (y * g.astype(jnp.float32) + b.astype(jnp.float32)).astype(x.dtype)


def neighbourhood_attention(q, k, v, rpb):
    b, s, h, d = q.shape
    rows = s // GRID_W
    kh = min(NA_KH_MAX, rows)
    scale = NA_HEAD_DIM ** -0.5
    qg = q.reshape(b, rows, GRID_W, h, d)
    kg = k.reshape(b, rows, GRID_W, h, d)
    vg = v.reshape(b, rows, GRID_W, h, d)

    cols = jnp.arange(GRID_W)
    col_start = jnp.clip(cols - NA_KW // 2, 0, GRID_W - NA_KW)
    col_mask = (cols[None, :] >= col_start[:, None]) & (cols[None, :] < col_start[:, None] + NA_KW)
    dc_idx = jnp.clip(cols[None, :] - cols[:, None], -(NA_KW - 1), NA_KW - 1) + (NA_KW - 1)
    col_bias = rpb.astype(jnp.float32)[:, :, dc_idx]

    def one_row(r):
        rs = jnp.clip(r - kh // 2, 0, rows - kh)
        q_r = lax.dynamic_index_in_dim(qg, r, axis=1, keepdims=False)
        k_blk = lax.dynamic_slice_in_dim(kg, rs, kh, axis=1)
        v_blk = lax.dynamic_slice_in_dim(vg, rs, kh, axis=1)
        dr_idx = rs + jnp.arange(kh) - r + (NA_KH_MAX - 1)
        bias = jnp.transpose(col_bias[:, dr_idx], (0, 2, 1, 3))
        sc = jnp.einsum('bqhd,bikhd->bhqik', q_r, k_blk,
                        preferred_element_type=jnp.float32) * scale + bias[None]
        sc = jnp.where(col_mask[:, None, :], sc, -jnp.inf)
        p = jax.nn.softmax(sc.reshape(b, h, GRID_W, kh * GRID_W), axis=-1)
        p = p.reshape(b, h, GRID_W, kh, GRID_W).astype(v.dtype)
        return jnp.einsum('bhqik,bikhd->bqhd', p, v_blk)

    out = lax.map(one_row, jnp.arange(rows))
    return jnp.transpose(out, (1, 0, 2, 3, 4)).reshape(b, s, h * d)


def spatial_gating(u, v, ln_g, ln_b, w_s, b_s):
    b, s, _ = v.shape
    n_chunks = s // SG_CHUNK
    v = layernorm(v, ln_g, ln_b)
    vc = v.reshape(b, n_chunks, SG_CHUNK, SG_GROUPS, SG_GROUP_DIM)
    mixed = jnp.einsum('gpq,bcqgd->bcpgd', w_s.astype(v.dtype), vc) \
        + jnp.transpose(b_s, (1, 0)).astype(v.dtype)[None, None, :, :, None]
    return u * mixed.reshape(b, s, SG_WIDTH)


def setup_inputs(seed: int = 0) -> dict:
    key = jax.random.key(seed)
    ks = jax.random.split(key, 17)
    f32 = jnp.float32

    def nrm(k, shape, scale):
        return jax.random.normal(k, shape, f32) * scale

    def gain(k, shape):
        return 1.0 + 0.05 * jax.random.normal(k, shape, f32)

    return {
        "x": jax.random.normal(ks[0], (BATCH, SEQ, D_MODEL), f32),
        "norm_mix_pre": gain(ks[1], (DEPTH, D_MODEL)),
        "w_in": nrm(ks[2], (DEPTH, D_MODEL, IN_COLS), D_MODEL ** -0.5),
        "na_rpb": nrm(ks[3], (DEPTH, NA_HEADS, 2 * NA_KH_MAX - 1, 2 * NA_KW - 1), 0.5),
        "sg_ln_g": gain(ks[4], (DEPTH, SG_WIDTH)),
        "sg_ln_b": nrm(ks[5], (DEPTH, SG_WIDTH), 0.02),
        "sg_w_s": nrm(ks[6], (DEPTH, SG_GROUPS, SG_CHUNK, SG_CHUNK), SG_CHUNK ** -0.5),
        "sg_b_s": gain(ks[7], (DEPTH, SG_GROUPS, SG_CHUNK)),
        "g_out_na": gain(ks[8], (DEPTH, NA_WIDTH)),
        "g_out_sg": gain(ks[9], (DEPTH, SG_WIDTH)),
        "w_out": nrm(ks[10], (DEPTH, MIX_WIDTH, D_MODEL), MIX_WIDTH ** -0.5),
        "norm_mix_post": gain(ks[11], (DEPTH, D_MODEL)),
        "norm_ffn_pre": gain(ks[12], (DEPTH, D_MODEL)),
        "w_ff1": nrm(ks[13], (DEPTH, D_MODEL, D_FF), D_MODEL ** -0.5),
        "w_ff2": nrm(ks[14], (DEPTH, D_FF, D_MODEL), D_FF ** -0.5),
        "norm_ffn_post": gain(ks[15], (DEPTH, D_MODEL)),
    }


def reference(x, norm_mix_pre, w_in, na_rpb, sg_ln_g, sg_ln_b, sg_w_s, sg_b_s,
              g_out_na, g_out_sg, w_out, norm_mix_post, norm_ffn_pre, w_ff1, w_ff2,
              norm_ffn_post):
    b, s, _ = x.shape
    splits = [NA_WIDTH, 2 * NA_WIDTH, 3 * NA_WIDTH, 3 * NA_WIDTH + SG_WIDTH]
    for l in range(DEPTH):
        h = rmsnorm(x, norm_mix_pre[l])
        proj = h @ w_in[l].astype(x.dtype)
        q, k, v, su, sv = jnp.split(proj, splits, axis=-1)
        q = q.reshape(b, s, NA_HEADS, NA_HEAD_DIM)
        k = k.reshape(b, s, NA_HEADS, NA_HEAD_DIM)
        v = v.reshape(b, s, NA_HEADS, NA_HEAD_DIM)
        attn = neighbourhood_attention(q, k, v, na_rpb[l])
        sgu = spatial_gating(jax.nn.gelu(su), jax.nn.gelu(sv), sg_ln_g[l], sg_ln_b[l],
                             sg_w_s[l], sg_b_s[l])
        mix = jnp.concatenate([rmsnorm(attn, g_out_na[l]), rmsnorm(sgu, g_out_sg[l])], axis=-1)
        x = x + rmsnorm(mix @ w_out[l].astype(x.dtype), norm_mix_post[l])
        h = rmsnorm(x, norm_ffn_pre[l])
        f = jnp.square(jax.nn.relu(h @ w_ff1[l].astype(x.dtype))) @ w_ff2[l].astype(x.dtype)
        x = x + rmsnorm(f, norm_ffn_post[l])
    return x
```

```python
import functools

import numpy as np
import jax
import jax.numpy as jnp
from jax import lax
from jax.experimental import pallas as pl
from jax.experimental.pallas import tpu as pltpu

D_MODEL = 1024
GRID_W = 64
NA_HEADS = 8
NA_HEAD_DIM = 64
NA_WIDTH = NA_HEADS * NA_HEAD_DIM
NA_KH = 8
NA_KW = 16
SG_GROUPS = 8
SG_GROUP_DIM = 64
SG_WIDTH = SG_GROUPS * SG_GROUP_DIM
SG_CHUNK = 128
D_FF = 4 * D_MODEL
EPS = 1e-6

LANES = 128
HEAD_PAIRS = NA_HEADS // 2
QBLK_ROWS = 4
QBLK = QBLK_ROWS * GRID_W
VMEM_LIMIT = 56 * 1024 * 1024

F32 = jnp.float32
BF16 = jnp.bfloat16


def _rms(x, g):
    return x * lax.rsqrt(jnp.mean(x * x, axis=-1, keepdims=True) + EPS) * g


def _proj_kernel(x_ref, gpre_ref, win_ref, lng_ref, lnb_ref, wcat_ref, bs_ref, gsg_ref,
                 qkv_ref, sgu_ref):
    tm = x_ref.shape[0]
    h = _rms(x_ref[...], gpre_ref[...]).astype(BF16)

    def proj(c0, width):
        return jnp.dot(h, win_ref[:, c0:c0 + width], preferred_element_type=F32)

    scale = NA_HEAD_DIM ** -0.5
    for i in range(3 * HEAD_PAIRS):
        p = proj(i * LANES, LANES)
        if i < HEAD_PAIRS:
            p = p * scale
        qkv_ref[i] = p.astype(BF16)

    su = jax.nn.gelu(proj(3 * NA_WIDTH, SG_WIDTH))
    sv = jax.nn.gelu(proj(3 * NA_WIDTH + SG_WIDTH, SG_WIDTH))
    mu = jnp.mean(sv, axis=-1, keepdims=True)
    xc = sv - mu
    v = xc * lax.rsqrt(jnp.mean(xc * xc, axis=-1, keepdims=True) + EPS)
    v = (v * lng_ref[...] + lnb_ref[...]).astype(BF16)

    lane = lax.broadcasted_iota(jnp.int32, (SG_CHUNK, LANES), 1)
    first = lane < SG_GROUP_DIM
    zero = jnp.zeros((SG_CHUNK, LANES), BF16)
    chunks = []
    for c in range(tm // SG_CHUNK):
        cols = []
        for gp in range(SG_GROUPS // 2):
            v128 = v[c * SG_CHUNK:(c + 1) * SG_CHUNK, gp * LANES:(gp + 1) * LANES]
            rhs = jnp.concatenate([jnp.where(first, v128, zero), jnp.where(first, zero, v128)], axis=0)
            cols.append(jnp.dot(wcat_ref[gp], rhs, preferred_element_type=F32))
        chunks.append(jnp.concatenate(cols, axis=1) + bs_ref[...])
    mixed = jnp.concatenate(chunks, axis=0)
    sgu_ref[...] = _rms(su * mixed, gsg_ref[...]).astype(BF16)


def _row_valid(r, kr, rows):
    rs = min(max(r - NA_KH // 2, 0), rows - NA_KH)
    return 0 <= kr < rows and rs <= kr < rs + NA_KH


def _bias_plan(rows):
    nblk = rows // QBLK_ROWS
    pieces = {}
    variants = {}

    def plan_for(j):
        kt0, kt1 = max(j - 1, 0), min(j + 1, nblk - 1)
        nk_rows = (kt1 - kt0 + 1) * QBLK_ROWS
        table = []
        for qr in range(QBLK_ROWS):
            r = j * QBLK_ROWS + qr
            row = []
            for p in range(nk_rows // 2):
                kr0 = kt0 * QBLK_ROWS + 2 * p
                key = (kr0 - r, _row_valid(r, kr0, rows), _row_valid(r, kr0 + 1, rows))
                row.append(pieces.setdefault(key, len(pieces)))
            table.append(tuple(row))
        return tuple(table)

    variants["first"] = plan_for(0)
    variants["last"] = plan_for(nblk - 1)
    interior = [plan_for(j) for j in range(1, nblk - 1)]
    assert all(t == interior[0] for t in interior)
    variants["interior"] = interior[0]
    return list(pieces.keys()), variants


def _bias_table(rpb, rows):
    keys, _ = _bias_plan(rows)
    qc = np.arange(GRID_W)[:, None]
    lane = np.arange(LANES)[None, :]
    kc = lane % GRID_W
    col_start = np.clip(qc - NA_KW // 2, 0, GRID_W - NA_KW)
    col_ok = (kc >= col_start) & (kc < col_start + NA_KW)
    dc_idx = np.clip(kc - qc, -(NA_KW - 1), NA_KW - 1) + (NA_KW - 1)
    dr_idx, ok = [], []
    for d, v0, v1 in keys:
        dr = np.where(lane < GRID_W, d, d + 1) + (NA_KH - 1)
        rv = np.where(lane < GRID_W, v0, v1)
        dr_idx.append(np.broadcast_to(np.clip(dr, 0, 2 * NA_KH - 2), (GRID_W, LANES)))
        ok.append(col_ok & rv)
    dr_idx = np.stack(dr_idx)
    ok = np.stack(ok)
    dc_idx = np.broadcast_to(dc_idx, dr_idx.shape)
    vals = rpb.astype(F32)[:, dr_idx, dc_idx]
    return jnp.where(ok[None], vals, -jnp.inf)


def _attn_kernel(qkv_ref, tab_ref, g_ref, o_ref, acc_ref, *, rows):
    _, variants = _bias_plan(rows)
    nblk = rows // QBLK_ROWS
    lane = lax.broadcasted_iota(jnp.int32, (QBLK, LANES), 1)
    first = lane < NA_HEAD_DIM

    def qblock(hp, q0, k0, plan):
        nk = len(plan[0]) * LANES
        q = qkv_ref[hp, pl.ds(q0, QBLK), :]
        k = qkv_ref[HEAD_PAIRS + hp, pl.ds(k0, nk), :]
        v = qkv_ref[2 * HEAD_PAIRS + hp, pl.ds(k0, nk), :]
        outs = []
        for hh in range(2):
            keep = first if hh == 0 else jnp.logical_not(first)
            qm = jnp.where(keep, q, jnp.zeros_like(q))
            s = lax.dot_general(qm, k, (((1,), (1,)), ((), ())), preferred_element_type=F32)
            bias = jnp.concatenate(
                [jnp.concatenate([tab_ref[2 * hp + hh, idx] for idx in row], axis=1) for row in plan],
                axis=0)
            s = s + bias
            m = jnp.max(s, axis=-1, keepdims=True)
            p = jnp.exp(s - m)
            l = jnp.sum(p, axis=-1, keepdims=True)
            o = jnp.dot(p.astype(BF16), v, preferred_element_type=F32)
            outs.append(o / l)
        acc_ref[hp, pl.ds(q0, QBLK), :] = jnp.where(first, outs[0], outs[1])

    def per_pair(hp, carry):
        qblock(hp, 0, 0, variants["first"])

        def mid(j, c):
            q0 = pl.multiple_of(j * QBLK, QBLK)
            k0 = pl.multiple_of((j - 1) * QBLK, QBLK)
            qblock(hp, q0, k0, variants["interior"])
            return c

        lax.fori_loop(1, nblk - 1, mid, 0)
        qblock(hp, (nblk - 1) * QBLK, (nblk - 2) * QBLK, variants["last"])
        return carry

    lax.fori_loop(0, HEAD_PAIRS, per_pair, 0)

    ss = jnp.zeros((acc_ref.shape[1], 1), F32)
    for hp in range(HEAD_PAIRS):
        a = acc_ref[hp]
        ss = ss + jnp.sum(a * a, axis=-1, keepdims=True)
    inv = lax.rsqrt(ss / NA_WIDTH + EPS)
    for hp in range(HEAD_PAIRS):
        o_ref[:, hp * LANES:(hp + 1) * LANES] = (
            acc_ref[hp] * inv * g_ref[:, hp * LANES:(hp + 1) * LANES]).astype(BF16)


def _ffn_kernel(x_ref, a_ref, s_ref, wout_ref, gpost_ref, gpre2_ref, w1_ref, w2_ref, gpost2_ref,
                o_ref, *, ff_chunk):
    y = jnp.dot(a_ref[...], wout_ref[:NA_WIDTH, :], preferred_element_type=F32)
    y = y + jnp.dot(s_ref[...], wout_ref[NA_WIDTH:, :], preferred_element_type=F32)
    x1 = x_ref[...] + _rms(y, gpost_ref[...])
    h = _rms(x1, gpre2_ref[...]).astype(BF16)
    f = jnp.zeros_like(x1)
    for c in range(D_FF // ff_chunk):
        a = jnp.dot(h, w1_ref[:, c * ff_chunk:(c + 1) * ff_chunk], preferred_element_type=F32)
        a = jnp.square(jnp.maximum(a, 0.0)).astype(BF16)
        f = f + jnp.dot(a, w2_ref[c * ff_chunk:(c + 1) * ff_chunk, :], preferred_element_type=F32)
    o_ref[...] = x1 + _rms(f, gpost2_ref[...])


def _const_spec(shape):
    nd = len(shape)
    return pl.BlockSpec(shape, lambda *_: (0,) * nd, pipeline_mode=pl.Buffered(1))


def _layer(x2, b, s, p):
    t = b * s
    rows = s // GRID_W
    tm_proj = 512
    tm_ffn = 512
    row = lambda a: a.reshape(1, -1).astype(F32)

    win = p["w_in"].astype(BF16)
    w_s = p["sg_w_s"].astype(BF16)
    wcat = jnp.concatenate([w_s[0::2], w_s[1::2]], axis=2)
    bs_full = jnp.repeat(p["sg_b_s"].astype(F32).T, SG_GROUP_DIM, axis=1)

    qkv, sgu = pl.pallas_call(
        _proj_kernel,
        grid=(t // tm_proj,),
        in_specs=[
            pl.BlockSpec((tm_proj, D_MODEL), lambda i: (i, 0)),
            _const_spec((1, D_MODEL)),
            _const_spec(win.shape),
            _const_spec((1, SG_WIDTH)),
            _const_spec((1, SG_WIDTH)),
            _const_spec(wcat.shape),
            _const_spec(bs_full.shape),
            _const_spec((1, SG_WIDTH)),
        ],
        out_specs=[
            pl.BlockSpec((3 * HEAD_PAIRS, tm_proj, LANES), lambda i: (0, i, 0)),
            pl.BlockSpec((tm_proj, SG_WIDTH), lambda i: (i, 0)),
        ],
        out_shape=[
            jax.ShapeDtypeStruct((3 * HEAD_PAIRS, t, LANES), BF16),
            jax.ShapeDtypeStruct((t, SG_WIDTH), BF16),
        ],
        compiler_params=pltpu.CompilerParams(
            dimension_semantics=("arbitrary",), vmem_limit_bytes=VMEM_LIMIT),
        name="proj_sgu",
    )(x2, row(p["norm_mix_pre"]), win, row(p["sg_ln_g"]), row(p["sg_ln_b"]), wcat, bs_full,
      row(p["g_out_sg"]))

    tab = _bias_table(p["na_rpb"], rows)
    attn = pl.pallas_call(
        functools.partial(_attn_kernel, rows=rows),
        grid=(b,),
        in_specs=[
            pl.BlockSpec((3 * HEAD_PAIRS, s, LANES), lambda i: (0, i, 0)),
            _const_spec(tab.shape),
            _const_spec((1, NA_WIDTH)),
        ],
        out_specs=pl.BlockSpec((s, NA_WIDTH), lambda i: (i, 0)),
        out_shape=jax.ShapeDtypeStruct((t, NA_WIDTH), BF16),
        scratch_shapes=[pltpu.VMEM((HEAD_PAIRS, s, LANES), F32)],
        compiler_params=pltpu.CompilerParams(
            dimension_semantics=("arbitrary",), vmem_limit_bytes=VMEM_LIMIT),
        name="natten",
    )(qkv, tab, row(p["g_out_na"]))

    wout = p["w_out"].astype(BF16)
    w1 = p["w_ff1"].astype(BF16)
    w2 = p["w_ff2"].astype(BF16)
    out = pl.pallas_call(
        functools.partial(_ffn_kernel, ff_chunk=1024),
        grid=(t // tm_ffn,),
        in_specs=[
            pl.BlockSpec((tm_ffn, D_MODEL), lambda i: (i, 0)),
            pl.BlockSpec((tm_ffn, NA_WIDTH), lambda i: (i, 0)),
            pl.BlockSpec((tm_ffn, SG_WIDTH), lambda i: (i, 0)),
            _const_spec(wout.shape),
            _const_spec((1, D_MODEL)),
            _const_spec((1, D_MODEL)),
            _const_spec(w1.shape),
            _const_spec(w2.shape),
            _const_spec((1, D_MODEL)),
        ],
        out_specs=pl.BlockSpec((tm_ffn, D_MODEL), lambda i: (i, 0)),
        out_shape=jax.ShapeDtypeStruct((t, D_MODEL), F32),
        compiler_params=pltpu.CompilerParams(
            dimension_semantics=("arbitrary",), vmem_limit_bytes=VMEM_LIMIT),
        name="outproj_ffn",
    )(x2, attn, sgu, wout, row(p["norm_mix_post"]), row(p["norm_ffn_pre"]), w1, w2,
      row(p["norm_ffn_post"]))
    return out


def kernel(x, norm_mix_pre, w_in, na_rpb, sg_ln_g, sg_ln_b, sg_w_s, sg_b_s, g_out_na, g_out_sg,
           w_out, norm_mix_post, norm_ffn_pre, w_ff1, w_ff2, norm_ffn_post):
    b, s, d = x.shape
    assert d == D_MODEL and s % (QBLK * 2) == 0 and s // GRID_W >= 2 * NA_KH
    params = dict(norm_mix_pre=norm_mix_pre, w_in=w_in, na_rpb=na_rpb, sg_ln_g=sg_ln_g,
                  sg_ln_b=sg_ln_b, sg_w_s=sg_w_s, sg_b_s=sg_b_s, g_out_na=g_out_na,
                  g_out_sg=g_out_sg, w_out=w_out, norm_mix_post=norm_mix_post,
                  norm_ffn_pre=norm_ffn_pre, w_ff1=w_ff1, w_ff2=w_ff2, norm_ffn_post=norm_ffn_post)
    x2 = x.reshape(b * s, d)
    for l in range(norm_mix_pre.shape[0]):
        x2 = _layer(x2, b, s, {k: v[l] for k, v in params.items()})
    return x2.reshape(b, s, d)
```

```python
import functools

import numpy as np
import jax
import jax.numpy as jnp
from jax import lax
from jax.experimental import pallas as pl
from jax.experimental.pallas import tpu as pltpu

D_MODEL = 1024
GRID_W = 64
NA_HEADS = 8
NA_HEAD_DIM = 64
NA_WIDTH = NA_HEADS * NA_HEAD_DIM
NA_KH = 8
NA_KW = 16
SG_GROUPS = 8
SG_GROUP_DIM = 64
SG_WIDTH = SG_GROUPS * SG_GROUP_DIM
SG_CHUNK = 128
D_FF = 4 * D_MODEL
EPS = 1e-6

LANES = 128
HEAD_PAIRS = NA_HEADS // 2
QBLK_ROWS = 4
QBLK = QBLK_ROWS * GRID_W
VMEM_LIMIT = 56 * 1024 * 1024

F32 = jnp.float32
BF16 = jnp.bfloat16


def _rms(x, g):
    return x * lax.rsqrt(jnp.mean(x * x, axis=-1, keepdims=True) + EPS) * g


def _proj_kernel(x_ref, gpre_ref, win_ref, lng_ref, lnb_ref, wcat_ref, bs_ref, gsg_ref,
                 qkv_ref, sgu_ref):
    tm = x_ref.shape[0]
    h = _rms(x_ref[...], gpre_ref[...]).astype(BF16)

    def proj(c0, width):
        return jnp.dot(h, win_ref[:, c0:c0 + width], preferred_element_type=F32)

    scale = NA_HEAD_DIM ** -0.5
    for i in range(3 * HEAD_PAIRS):
        p = proj(i * LANES, LANES)
        if i < HEAD_PAIRS:
            p = p * scale
        qkv_ref[i] = p.astype(BF16)

    su = jax.nn.gelu(proj(3 * NA_WIDTH, SG_WIDTH))
    sv = jax.nn.gelu(proj(3 * NA_WIDTH + SG_WIDTH, SG_WIDTH))
    mu = jnp.mean(sv, axis=-1, keepdims=True)
    xc = sv - mu
    v = xc * lax.rsqrt(jnp.mean(xc * xc, axis=-1, keepdims=True) + EPS)
    v = (v * lng_ref[...] + lnb_ref[...]).astype(BF16)

    lane = lax.broadcasted_iota(jnp.int32, (SG_CHUNK, LANES), 1)
    first = lane < SG_GROUP_DIM
    zero = jnp.zeros((SG_CHUNK, LANES), BF16)
    chunks = []
    for c in range(tm // SG_CHUNK):
        cols = []
        for gp in range(SG_GROUPS // 2):
            v128 = v[c * SG_CHUNK:(c + 1) * SG_CHUNK, gp * LANES:(gp + 1) * LANES]
            rhs = jnp.concatenate([jnp.where(first, v128, zero), jnp.where(first, zero, v128)], axis=0)
            cols.append(jnp.dot(wcat_ref[gp], rhs, preferred_element_type=F32))
        chunks.append(jnp.concatenate(cols, axis=1) + bs_ref[...])
    mixed = jnp.concatenate(chunks, axis=0)
    sgu_ref[...] = _rms(su * mixed, gsg_ref[...]).astype(BF16)


def _row_valid(r, kr, rows):
    rs = min(max(r - NA_KH // 2, 0), rows - NA_KH)
    return 0 <= kr < rows and rs <= kr < rs + NA_KH


def _bias_plan(rows):
    nblk = rows // QBLK_ROWS
    pieces = {}
    variants = {}

    def plan_for(j):
        kt0, kt1 = max(j - 1, 0), min(j + 1, nblk - 1)
        nk_rows = (kt1 - kt0 + 1) * QBLK_ROWS
        table = []
        for qr in range(QBLK_ROWS):
            r = j * QBLK_ROWS + qr
            row = []
            for p in range(nk_rows // 2):
                kr0 = kt0 * QBLK_ROWS + 2 * p
                key = (kr0 - r, _row_valid(r, kr0, rows), _row_valid(r, kr0 + 1, rows))
                row.append(pieces.setdefault(key, len(pieces)))
            table.append(tuple(row))
        return tuple(table)

    variants["first"] = plan_for(0)
    variants["last"] = plan_for(nblk - 1)
    interior = [plan_for(j) for j in range(1, nblk - 1)]
    assert all(t == interior[0] for t in interior)
    variants["interior"] = interior[0]
    return list(pieces.keys()), variants


def _bias_table(rpb, rows):
    keys, _ = _bias_plan(rows)
    rpb = rpb.astype(F32)
    h, ndr, ndc = rpb.shape
    pad = GRID_W - NA_KW
    ext = jnp.concatenate([jnp.broadcast_to(rpb[..., :1], (h, ndr, pad)), rpb,
                           jnp.broadcast_to(rpb[..., -1:], (h, ndr, pad))], axis=-1)
    toep = jnp.stack([ext[..., GRID_W - 1 - qc:2 * GRID_W - 1 - qc] for qc in range(GRID_W)],
                     axis=2)
    qc = np.arange(GRID_W)[:, None]
    kc = np.arange(GRID_W)[None, :]
    col_start = np.clip(qc - NA_KW // 2, 0, GRID_W - NA_KW)
    col_ok = (kc >= col_start) & (kc < col_start + NA_KW)
    masked = jnp.where(col_ok, toep, -jnp.inf)
    dead = jnp.full((h, GRID_W, GRID_W), -jnp.inf, F32)
    pieces = []
    for d, v0, v1 in keys:
        halves = [masked[:, dd + NA_KH - 1] if ok else dead for dd, ok in ((d, v0), (d + 1, v1))]
        pieces.append(jnp.concatenate(halves, axis=-1))
    return jnp.stack(pieces, axis=1)


def _attn_kernel(qkv_ref, tab_ref, g_ref, o_ref, acc_ref, *, rows):
    _, variants = _bias_plan(rows)
    nblk = rows // QBLK_ROWS
    lane = lax.broadcasted_iota(jnp.int32, (QBLK, LANES), 1)
    first = lane < NA_HEAD_DIM

    def qblock(hp, q0, k0, plan):
        nk = len(plan[0]) * LANES
        q = qkv_ref[hp, pl.ds(q0, QBLK), :]
        k = qkv_ref[HEAD_PAIRS + hp, pl.ds(k0, nk), :]
        v = qkv_ref[2 * HEAD_PAIRS + hp, pl.ds(k0, nk), :]
        outs = []
        for hh in range(2):
            keep = first if hh == 0 else jnp.logical_not(first)
            qm = jnp.where(keep, q, jnp.zeros_like(q))
            s = lax.dot_general(qm, k, (((1,), (1,)), ((), ())), preferred_element_type=F32)
            bias = jnp.concatenate(
                [jnp.concatenate([tab_ref[2 * hp + hh, idx] for idx in row], axis=1) for row in plan],
                axis=0)
            s = s + bias
            m = jnp.max(s, axis=-1, keepdims=True)
            p = jnp.exp(s - m)
            l = jnp.sum(p, axis=-1, keepdims=True)
            o = jnp.dot(p.astype(BF16), v, preferred_element_type=F32)
            outs.append(o / l)
        acc_ref[hp, pl.ds(q0, QBLK), :] = jnp.where(first, outs[0], outs[1])

    def per_pair(hp, carry):
        qblock(hp, 0, 0, variants["first"])

        def mid(j, c):
            q0 = pl.multiple_of(j * QBLK, QBLK)
            k0 = pl.multiple_of((j - 1) * QBLK, QBLK)
            qblock(hp, q0, k0, variants["interior"])
            return c

        lax.fori_loop(1, nblk - 1, mid, 0)
        qblock(hp, (nblk - 1) * QBLK, (nblk - 2) * QBLK, variants["last"])
        return carry

    lax.fori_loop(0, HEAD_PAIRS, per_pair, 0)

    ss = jnp.zeros((acc_ref.shape[1], 1), F32)
    for hp in range(HEAD_PAIRS):
        a = acc_ref[hp]
        ss = ss + jnp.sum(a * a, axis=-1, keepdims=True)
    inv = lax.rsqrt(ss / NA_WIDTH + EPS)
    for hp in range(HEAD_PAIRS):
        o_ref[:, hp * LANES:(hp + 1) * LANES] = (
            acc_ref[hp] * inv * g_ref[:, hp * LANES:(hp + 1) * LANES]).astype(BF16)


def _ffn_kernel(x_ref, a_ref, s_ref, wout_ref, gpost_ref, gpre2_ref, w1_ref, w2_ref, gpost2_ref,
                o_ref, *, ff_chunk):
    y = jnp.dot(a_ref[...], wout_ref[:NA_WIDTH, :], preferred_element_type=F32)
    y = y + jnp.dot(s_ref[...], wout_ref[NA_WIDTH:, :], preferred_element_type=F32)
    x1 = x_ref[...] + _rms(y, gpost_ref[...])
    h = _rms(x1, gpre2_ref[...]).astype(BF16)
    f = jnp.zeros_like(x1)
    for c in range(D_FF // ff_chunk):
        a = jnp.dot(h, w1_ref[:, c * ff_chunk:(c + 1) * ff_chunk], preferred_element_type=F32)
        a = jnp.square(jnp.maximum(a, 0.0)).astype(BF16)
        f = f + jnp.dot(a, w2_ref[c * ff_chunk:(c + 1) * ff_chunk, :], preferred_element_type=F32)
    o_ref[...] = x1 + _rms(f, gpost2_ref[...])


def _const_spec(shape):
    nd = len(shape)
    return pl.BlockSpec(shape, lambda *_: (0,) * nd, pipeline_mode=pl.Buffered(1))


def _layer(x2, b, s, p):
    t = b * s
    rows = s // GRID_W
    tm_proj = 512
    tm_ffn = 512
    row = lambda a: a.reshape(1, -1).astype(F32)

    win = p["w_in"].astype(BF16)
    w_s = p["sg_w_s"].astype(BF16)
    wcat = jnp.concatenate([w_s[0::2], w_s[1::2]], axis=2)
    bs_full = jnp.repeat(p["sg_b_s"].astype(F32).T, SG_GROUP_DIM, axis=1)

    qkv, sgu = pl.pallas_call(
        _proj_kernel,
        grid=(t // tm_proj,),
        in_specs=[
            pl.BlockSpec((tm_proj, D_MODEL), lambda i: (i, 0)),
            _const_spec((1, D_MODEL)),
            _const_spec(win.shape),
            _const_spec((1, SG_WIDTH)),
            _const_spec((1, SG_WIDTH)),
            _const_spec(wcat.shape),
            _const_spec(bs_full.shape),
            _const_spec((1, SG_WIDTH)),
        ],
        out_specs=[
            pl.BlockSpec((3 * HEAD_PAIRS, tm_proj, LANES), lambda i: (0, i, 0)),
            pl.BlockSpec((tm_proj, SG_WIDTH), lambda i: (i, 0)),
        ],
        out_shape=[
            jax.ShapeDtypeStruct((3 * HEAD_PAIRS, t, LANES), BF16),
            jax.ShapeDtypeStruct((t, SG_WIDTH), BF16),
        ],
        compiler_params=pltpu.CompilerParams(
            dimension_semantics=("arbitrary",), vmem_limit_bytes=VMEM_LIMIT),
        name="proj_sgu",
    )(x2, row(p["norm_mix_pre"]), win, row(p["sg_ln_g"]), row(p["sg_ln_b"]), wcat, bs_full,
      row(p["g_out_sg"]))

    tab = _bias_table(p["na_rpb"], rows)
    attn = pl.pallas_call(
        functools.partial(_attn_kernel, rows=rows),
        grid=(b,),
        in_specs=[
            pl.BlockSpec((3 * HEAD_PAIRS, s, LANES), lambda i: (0, i, 0)),
            _const_spec(tab.shape),
            _const_spec((1, NA_WIDTH)),
        ],
        out_specs=pl.BlockSpec((s, NA_WIDTH), lambda i: (i, 0)),
        out_shape=jax.ShapeDtypeStruct((t, NA_WIDTH), BF16),
        scratch_shapes=[pltpu.VMEM((HEAD_PAIRS, s, LANES), F32)],
        compiler_params=pltpu.CompilerParams(
            dimension_semantics=("arbitrary",), vmem_limit_bytes=VMEM_LIMIT),
        name="natten",
    )(qkv, tab, row(p["g_out_na"]))

    wout = p["w_out"].astype(BF16)
    w1 = p["w_ff1"].astype(BF16)
    w2 = p["w_ff2"].astype(BF16)
    out = pl.pallas_call(
        functools.partial(_ffn_kernel, ff_chunk=1024),
        grid=(t // tm_ffn,),
        in_specs=[
            pl.BlockSpec((tm_ffn, D_MODEL), lambda i: (i, 0)),
            pl.BlockSpec((tm_ffn, NA_WIDTH), lambda i: (i, 0)),
            pl.BlockSpec((tm_ffn, SG_WIDTH), lambda i: (i, 0)),
            _const_spec(wout.shape),
            _const_spec((1, D_MODEL)),
            _const_spec((1, D_MODEL)),
            _const_spec(w1.shape),
            _const_spec(w2.shape),
            _const_spec((1, D_MODEL)),
        ],
        out_specs=pl.BlockSpec((tm_ffn, D_MODEL), lambda i: (i, 0)),
        out_shape=jax.ShapeDtypeStruct((t, D_MODEL), F32),
        compiler_params=pltpu.CompilerParams(
            dimension_semantics=("arbitrary",), vmem_limit_bytes=VMEM_LIMIT),
        name="outproj_ffn",
    )(x2, attn, sgu, wout, row(p["norm_mix_post"]), row(p["norm_ffn_pre"]), w1, w2,
      row(p["norm_ffn_post"]))
    return out


def kernel(x, norm_mix_pre, w_in, na_rpb, sg_ln_g, sg_ln_b, sg_w_s, sg_b_s, g_out_na, g_out_sg,
           w_out, norm_mix_post, norm_ffn_pre, w_ff1, w_ff2, norm_ffn_post):
    b, s, d = x.shape
    assert d == D_MODEL and s % (QBLK * 2) == 0 and s // GRID_W >= 2 * NA_KH
    params = dict(norm_mix_pre=norm_mix_pre, w_in=w_in, na_rpb=na_rpb, sg_ln_g=sg_ln_g,
                  sg_ln_b=sg_ln_b, sg_w_s=sg_w_s, sg_b_s=sg_b_s, g_out_na=g_out_na,
                  g_out_sg=g_out_sg, w_out=w_out, norm_mix_post=norm_mix_post,
                  norm_ffn_pre=norm_ffn_pre, w_ff1=w_ff1, w_ff2=w_ff2, norm_ffn_post=norm_ffn_post)
    x2 = x.reshape(b * s, d)
    for l in range(norm_mix_pre.shape[0]):
        x2 = _layer(x2, b, s, {k: v[l] for k, v in params.items()})
    return x2.reshape(b, s, d)
```

```python
import functools

import numpy as np
import jax
import jax.numpy as jnp
from jax import lax
from jax.experimental import pallas as pl
from jax.experimental.pallas import tpu as pltpu

D_MODEL = 1024
GRID_W = 64
NA_HEADS = 8
NA_HEAD_DIM = 64
NA_WIDTH = NA_HEADS * NA_HEAD_DIM
NA_KH = 8
NA_KW = 16
SG_GROUPS = 8
SG_GROUP_DIM = 64
SG_WIDTH = SG_GROUPS * SG_GROUP_DIM
SG_CHUNK = 128
D_FF = 4 * D_MODEL
EPS = 1e-6

LANES = 128
HEAD_PAIRS = NA_HEADS // 2
QBLK_ROWS = 4
QBLK = QBLK_ROWS * GRID_W
BAND_TILES = 3
BAND = BAND_TILES * QBLK
BAND_PAIRS = BAND // LANES
SM_ROWS = 32
SUB_ROWS = 512
VMEM_LIMIT = 56 * 1024 * 1024

F32 = jnp.float32
BF16 = jnp.bfloat16


def _rms(x, g):
    return x * lax.rsqrt(jnp.mean(x * x, axis=-1, keepdims=True) + EPS) * g


def _proj_kernel(x_ref, gpre_ref, win_ref, lng_ref, lnb_ref, wcat_ref, bs_ref, gsg_ref,
                 qkv_ref, sgu_ref):
    for r0 in range(0, x_ref.shape[0], SUB_ROWS):
        _proj_rows(r0, x_ref, gpre_ref, win_ref, lng_ref, lnb_ref, wcat_ref, bs_ref, gsg_ref,
                   qkv_ref, sgu_ref)


def _proj_rows(r0, x_ref, gpre_ref, win_ref, lng_ref, lnb_ref, wcat_ref, bs_ref, gsg_ref,
               qkv_ref, sgu_ref):
    rows = slice(r0, r0 + SUB_ROWS)
    h = _rms(x_ref[rows, :], gpre_ref[...]).astype(BF16)

    def proj(c0, width):
        return jnp.dot(h, win_ref[:, c0:c0 + width], preferred_element_type=F32)

    scale = NA_HEAD_DIM ** -0.5
    for part in range(3):
        p = proj(part * NA_WIDTH, NA_WIDTH)
        if part == 0:
            p = p * scale
        for hp in range(HEAD_PAIRS):
            qkv_ref[part * HEAD_PAIRS + hp, rows, :] = p[:, hp * LANES:(hp + 1) * LANES].astype(BF16)

    su = jax.nn.gelu(proj(3 * NA_WIDTH, SG_WIDTH))
    sv = jax.nn.gelu(proj(3 * NA_WIDTH + SG_WIDTH, SG_WIDTH))
    mu = jnp.mean(sv, axis=-1, keepdims=True)
    xc = sv - mu
    v = xc * lax.rsqrt(jnp.mean(xc * xc, axis=-1, keepdims=True) + EPS)
    v = (v * lng_ref[...] + lnb_ref[...]).astype(BF16)

    lane = lax.broadcasted_iota(jnp.int32, (SG_CHUNK, LANES), 1)
    first = lane < SG_GROUP_DIM
    zero = jnp.zeros((SG_CHUNK, LANES), BF16)
    chunks = []
    for c in range(SUB_ROWS // SG_CHUNK):
        cols = []
        for gp in range(SG_GROUPS // 2):
            v128 = v[c * SG_CHUNK:(c + 1) * SG_CHUNK, gp * LANES:(gp + 1) * LANES]
            rhs = jnp.concatenate([jnp.where(first, v128, zero), jnp.where(first, zero, v128)], axis=0)
            cols.append(jnp.dot(wcat_ref[gp], rhs, preferred_element_type=F32))
        chunks.append(jnp.concatenate(cols, axis=1) + bs_ref[...])
    mixed = jnp.concatenate(chunks, axis=0)
    sgu_ref[rows, :] = _rms(su * mixed, gsg_ref[...]).astype(BF16)


def _row_valid(r, kr, rows):
    rs = min(max(r - NA_KH // 2, 0), rows - NA_KH)
    return 0 <= kr < rows and rs <= kr < rs + NA_KH


def _bias_plan(rows):
    nblk = rows // QBLK_ROWS
    pieces = {}
    plan = []
    for j in range(nblk):
        kt0 = min(max(j - 1, 0), nblk - BAND_TILES)
        table = []
        for qr in range(QBLK_ROWS):
            r = j * QBLK_ROWS + qr
            for p in range(BAND_PAIRS):
                kr0 = kt0 * QBLK_ROWS + 2 * p
                key = (kr0 - r, _row_valid(r, kr0, rows), _row_valid(r, kr0 + 1, rows))
                table.append(pieces.setdefault(key, len(pieces)))
        plan.append(table)
    return list(pieces.keys()), np.asarray(plan, np.int32)


def _bias_table(rpb, rows):
    keys, _ = _bias_plan(rows)
    rpb = rpb.astype(F32)
    h, ndr, ndc = rpb.shape
    pad = GRID_W - NA_KW
    ext = jnp.concatenate([jnp.broadcast_to(rpb[..., :1], (h, ndr, pad)), rpb,
                           jnp.broadcast_to(rpb[..., -1:], (h, ndr, pad))], axis=-1)
    toep = jnp.stack([ext[..., GRID_W - 1 - qc:2 * GRID_W - 1 - qc] for qc in range(GRID_W)],
                     axis=2)
    qc = np.arange(GRID_W)[:, None]
    kc = np.arange(GRID_W)[None, :]
    col_start = np.clip(qc - NA_KW // 2, 0, GRID_W - NA_KW)
    col_ok = (kc >= col_start) & (kc < col_start + NA_KW)
    masked = jnp.where(col_ok, toep, -jnp.inf)
    dead = jnp.full((h, GRID_W, GRID_W), -jnp.inf, F32)
    pieces = []
    for d, v0, v1 in keys:
        halves = [masked[:, dd + NA_KH - 1] if ok else dead for dd, ok in ((d, v0), (d + 1, v1))]
        pieces.append(jnp.concatenate(halves, axis=-1))
    return jnp.stack(pieces, axis=1)


def _attn_kernel(plan_ref, qkv_ref, tab_ref, o_ref, s0_ref, s1_ref, p0_ref, p1_ref, l0_ref, l1_ref,
                 m_ref, *, rows):
    nblk = rows // QBLK_ROWS
    n_items = HEAD_PAIRS * nblk
    lane = lax.broadcasted_iota(jnp.int32, (QBLK, LANES), 1)
    first = lane < NA_HEAD_DIM

    def coords(i):
        hp = i // nblk
        j = i - hp * nblk
        q0 = pl.multiple_of(j * QBLK, QBLK)
        k0 = pl.multiple_of(jnp.clip(j - 1, 0, nblk - BAND_TILES) * QBLK, QBLK)
        return hp, j, q0, k0

    def scores(i, s_ref):
        hp, _, q0, k0 = coords(i)
        q = qkv_ref[hp, pl.ds(q0, QBLK), :]
        k = qkv_ref[HEAD_PAIRS + hp, pl.ds(k0, BAND), :]
        zero = jnp.zeros_like(q)
        qq = jnp.concatenate([jnp.where(first, q, zero), jnp.where(first, zero, q)], axis=0)
        s_ref[...] = lax.dot_general(qq, k, (((1,), (1,)), ((), ())), preferred_element_type=F32)

    def softmax(i, s_ref, p_ref, l_ref):
        hp, j, _, _ = coords(i)
        chunks_per_row = GRID_W // SM_ROWS

        def bias_chunk(r):
            hh, rem = divmod(r, QBLK // SM_ROWS)
            qr, part = divmod(rem, chunks_per_row)
            return jnp.concatenate(
                [tab_ref[2 * hp + hh, plan_ref[(j * QBLK_ROWS + qr) * BAND_PAIRS + p],
                         part * SM_ROWS:(part + 1) * SM_ROWS, :] for p in range(BAND_PAIRS)], axis=1)

        ms = []
        for r in range(2 * QBLK // SM_ROWS):
            sl = s_ref[r * SM_ROWS:(r + 1) * SM_ROWS, :] + bias_chunk(r)
            ms.append(jnp.max(sl, axis=-1, keepdims=True))
        for r in range(2 * QBLK // SM_ROWS):
            shift = bias_chunk(r) - ms[r]
            e = jnp.exp(s_ref[r * SM_ROWS:(r + 1) * SM_ROWS, :] + shift)
            l_ref[r * SM_ROWS:(r + 1) * SM_ROWS, :] = jnp.sum(e, axis=-1, keepdims=True)
            p_ref[r * SM_ROWS:(r + 1) * SM_ROWS, :] = e.astype(BF16)

    def pv(i, p_ref, l_ref):
        hp, _, q0, k0 = coords(i)
        v = qkv_ref[2 * HEAD_PAIRS + hp, pl.ds(k0, BAND), :]
        o = jnp.dot(p_ref[...], v, preferred_element_type=F32) / l_ref[...]
        o_ref[hp, pl.ds(q0, QBLK), :] = jnp.where(first, o[:QBLK], o[QBLK:]).astype(BF16)

    scores(0, s0_ref)
    scores(1, s1_ref)
    softmax(0, s0_ref, p0_ref, l0_ref)

    def body(t, carry):
        i = 2 * t
        pv(i, p0_ref, l0_ref)
        scores(i + 2, s0_ref)
        softmax(i + 1, s1_ref, p1_ref, l1_ref)
        pv(i + 1, p1_ref, l1_ref)
        scores(i + 3, s1_ref)
        softmax(i + 2, s0_ref, p0_ref, l0_ref)
        return carry

    lax.fori_loop(0, n_items // 2 - 1, body, 0)
    pv(n_items - 2, p0_ref, l0_ref)
    softmax(n_items - 1, s1_ref, p1_ref, l1_ref)
    pv(n_items - 1, p1_ref, l1_ref)


def _ffn_kernel(x_ref, a_ref, s_ref, gna_ref, wout_ref, gpost_ref, gpre2_ref, w1_ref, w2_ref,
                gpost2_ref, o_ref, *, ff_chunk):
    for r0 in range(0, x_ref.shape[0], SUB_ROWS):
        rows = slice(r0, r0 + SUB_ROWS)
        attn = jnp.concatenate([a_ref[hp, rows, :] for hp in range(HEAD_PAIRS)], axis=1).astype(F32)
        y = jnp.dot(_rms(attn, gna_ref[...]).astype(BF16), wout_ref[:NA_WIDTH, :],
                    preferred_element_type=F32)
        y = y + jnp.dot(s_ref[rows, :], wout_ref[NA_WIDTH:, :], preferred_element_type=F32)
        x1 = x_ref[rows, :] + _rms(y, gpost_ref[...])
        h = _rms(x1, gpre2_ref[...]).astype(BF16)
        f = jnp.zeros_like(x1)
        for c in range(D_FF // ff_chunk):
            a = jnp.dot(h, w1_ref[:, c * ff_chunk:(c + 1) * ff_chunk], preferred_element_type=F32)
            a = jnp.square(jnp.maximum(a, 0.0)).astype(BF16)
            f = f + jnp.dot(a, w2_ref[c * ff_chunk:(c + 1) * ff_chunk, :],
                            preferred_element_type=F32)
        o_ref[rows, :] = x1 + _rms(f, gpost2_ref[...])


def _const_spec(shape):
    nd = len(shape)
    return pl.BlockSpec(shape, lambda *_: (0,) * nd, pipeline_mode=pl.Buffered(1))


def _layer(x2, b, s, p):
    t = b * s
    rows = s // GRID_W
    tm_proj = 2 * SUB_ROWS
    tm_ffn = 2 * SUB_ROWS
    row = lambda a: a.reshape(1, -1).astype(F32)

    win = p["w_in"].astype(BF16)
    w_s = p["sg_w_s"].astype(BF16)
    wcat = jnp.concatenate([w_s[0::2], w_s[1::2]], axis=2)
    bs_full = jnp.repeat(p["sg_b_s"].astype(F32).T, SG_GROUP_DIM, axis=1)

    qkv, sgu = pl.pallas_call(
        _proj_kernel,
        grid=(t // tm_proj,),
        in_specs=[
            pl.BlockSpec((tm_proj, D_MODEL), lambda i: (i, 0)),
            _const_spec((1, D_MODEL)),
            _const_spec(win.shape),
            _const_spec((1, SG_WIDTH)),
            _const_spec((1, SG_WIDTH)),
            _const_spec(wcat.shape),
            _const_spec(bs_full.shape),
            _const_spec((1, SG_WIDTH)),
        ],
        out_specs=[
            pl.BlockSpec((3 * HEAD_PAIRS, tm_proj, LANES), lambda i: (0, i, 0)),
            pl.BlockSpec((tm_proj, SG_WIDTH), lambda i: (i, 0)),
        ],
        out_shape=[
            jax.ShapeDtypeStruct((3 * HEAD_PAIRS, t, LANES), BF16),
            jax.ShapeDtypeStruct((t, SG_WIDTH), BF16),
        ],
        compiler_params=pltpu.CompilerParams(
            dimension_semantics=("arbitrary",), vmem_limit_bytes=VMEM_LIMIT),
        name="proj_sgu",
    )(x2, row(p["norm_mix_pre"]), win, row(p["sg_ln_g"]), row(p["sg_ln_b"]), wcat, bs_full,
      row(p["g_out_sg"]))

    tab = _bias_table(p["na_rpb"], rows)
    plan = jnp.asarray(_bias_plan(rows)[1].reshape(-1))
    attn = pl.pallas_call(
        functools.partial(_attn_kernel, rows=rows),
        grid_spec=pltpu.PrefetchScalarGridSpec(
            num_scalar_prefetch=1,
            grid=(b,),
            in_specs=[
                pl.BlockSpec((3 * HEAD_PAIRS, s, LANES), lambda i, _: (0, i, 0)),
                _const_spec(tab.shape),
            ],
            out_specs=pl.BlockSpec((HEAD_PAIRS, s, LANES), lambda i, _: (0, i, 0)),
            scratch_shapes=[
                pltpu.VMEM((2 * QBLK, BAND), F32),
                pltpu.VMEM((2 * QBLK, BAND), F32),
                pltpu.VMEM((2 * QBLK, BAND), BF16),
                pltpu.VMEM((2 * QBLK, BAND), BF16),
                pltpu.VMEM((2 * QBLK, 1), F32),
                pltpu.VMEM((2 * QBLK, 1), F32),
                pltpu.VMEM((2 * QBLK, 1), F32),
            ],
        ),
        out_shape=jax.ShapeDtypeStruct((HEAD_PAIRS, t, LANES), BF16),
        compiler_params=pltpu.CompilerParams(
            dimension_semantics=("arbitrary",), vmem_limit_bytes=VMEM_LIMIT),
        name="natten",
    )(plan, qkv, tab)

    wout = p["w_out"].astype(BF16)
    w1 = p["w_ff1"].astype(BF16)
    w2 = p["w_ff2"].astype(BF16)
    out = pl.pallas_call(
        functools.partial(_ffn_kernel, ff_chunk=1024),
        grid=(t // tm_ffn,),
        in_specs=[
            pl.BlockSpec((tm_ffn, D_MODEL), lambda i: (i, 0)),
            pl.BlockSpec((HEAD_PAIRS, tm_ffn, LANES), lambda i: (0, i, 0)),
            pl.BlockSpec((tm_ffn, SG_WIDTH), lambda i: (i, 0)),
            _const_spec((1, NA_WIDTH)),
            _const_spec(wout.shape),
            _const_spec((1, D_MODEL)),
            _const_spec((1, D_MODEL)),
            _const_spec(w1.shape),
            _const_spec(w2.shape),
            _const_spec((1, D_MODEL)),
        ],
        out_specs=pl.BlockSpec((tm_ffn, D_MODEL), lambda i: (i, 0)),
        out_shape=jax.ShapeDtypeStruct((t, D_MODEL), F32),
        compiler_params=pltpu.CompilerParams(
            dimension_semantics=("arbitrary",), vmem_limit_bytes=VMEM_LIMIT),
        name="outproj_ffn",
    )(x2, attn, sgu, row(p["g_out_na"]), wout, row(p["norm_mix_post"]), row(p["norm_ffn_pre"]),
      w1, w2, row(p["norm_ffn_post"]))
    return out


def kernel(x, norm_mix_pre, w_in, na_rpb, sg_ln_g, sg_ln_b, sg_w_s, sg_b_s, g_out_na, g_out_sg,
           w_out, norm_mix_post, norm_ffn_pre, w_ff1, w_ff2, norm_ffn_post):
    b, s, d = x.shape
    assert d == D_MODEL and s % (QBLK * 2) == 0 and s // GRID_W >= 2 * NA_KH
    params = dict(norm_mix_pre=norm_mix_pre, w_in=w_in, na_rpb=na_rpb, sg_ln_g=sg_ln_g,
                  sg_ln_b=sg_ln_b, sg_w_s=sg_w_s, sg_b_s=sg_b_s, g_out_na=g_out_na,
                  g_out_sg=g_out_sg, w_out=w_out, norm_mix_post=norm_mix_post,
                  norm_ffn_pre=norm_ffn_pre, w_ff1=w_ff1, w_ff2=w_ff2, norm_ffn_post=norm_ffn_post)
    x2 = x.reshape(b * s, d)
    for l in range(norm_mix_pre.shape[0]):
        x2 = _layer(x2, b, s, {k: v[l] for k, v in params.items()})
    return x2.reshape(b, s, d)
```

```python
import functools

import numpy as np
import jax
import jax.numpy as jnp
from jax import lax
from jax.experimental import pallas as pl
from jax.experimental.pallas import tpu as pltpu

D_MODEL = 1024
GRID_W = 64
NA_HEADS = 8
NA_HEAD_DIM = 64
NA_WIDTH = NA_HEADS * NA_HEAD_DIM
NA_KH = 8
NA_KW = 16
SG_GROUPS = 8
SG_GROUP_DIM = 64
SG_WIDTH = SG_GROUPS * SG_GROUP_DIM
SG_CHUNK = 128
D_FF = 4 * D_MODEL
EPS = 1e-6

LANES = 128
HEAD_PAIRS = NA_HEADS // 2
QBLK_ROWS = 4
QBLK = QBLK_ROWS * GRID_W
BAND_TILES = 3
BAND = BAND_TILES * QBLK
BAND_PAIRS = BAND // LANES
SM_ROWS = 32
COL_HALVES = 2
CB_W = GRID_W // COL_HALVES
SUBLANES = 8
_KEY_RUNS = tuple((kr * GRID_W + cb * CB_W, cb * LANES + kr * CB_W)
                  for kr in range(QBLK_ROWS) for cb in range(COL_HALVES))
SUB_ROWS = 512
VMEM_LIMIT = 56 * 1024 * 1024

F32 = jnp.float32
BF16 = jnp.bfloat16


def _rms(x, g):
    return x * lax.rsqrt(jnp.mean(x * x, axis=-1, keepdims=True) + EPS) * g


def _proj_kernel(x_ref, gpre_ref, win_ref, lng_ref, lnb_ref, wcat_ref, bs_ref, gsg_ref,
                 qkv_ref, sgu_ref):
    for r0 in range(0, x_ref.shape[0], SUB_ROWS):
        _proj_rows(r0, x_ref, gpre_ref, win_ref, lng_ref, lnb_ref, wcat_ref, bs_ref, gsg_ref,
                   qkv_ref, sgu_ref)


def _proj_rows(r0, x_ref, gpre_ref, win_ref, lng_ref, lnb_ref, wcat_ref, bs_ref, gsg_ref,
               qkv_ref, sgu_ref):
    rows = slice(r0, r0 + SUB_ROWS)
    h = _rms(x_ref[rows, :], gpre_ref[...]).astype(BF16)

    def proj(c0, width):
        return jnp.dot(h, win_ref[:, c0:c0 + width], preferred_element_type=F32)

    scale = NA_HEAD_DIM ** -0.5
    for part in range(3):
        p = proj(part * NA_WIDTH, NA_WIDTH)
        if part == 0:
            p = p * scale
        for hp in range(HEAD_PAIRS):
            cols = p[:, hp * LANES:(hp + 1) * LANES].astype(BF16)
            if part == 0:
                qkv_ref[hp, rows, :] = cols
                continue
            for src, dst in _KEY_RUNS:
                for t0 in range(0, SUB_ROWS, QBLK):
                    qkv_ref[part * HEAD_PAIRS + hp, pl.ds(r0 + t0 + dst, CB_W), :] = (
                        cols[t0 + src:t0 + src + CB_W, :])

    su = jax.nn.gelu(proj(3 * NA_WIDTH, SG_WIDTH))
    sv = jax.nn.gelu(proj(3 * NA_WIDTH + SG_WIDTH, SG_WIDTH))
    mu = jnp.mean(sv, axis=-1, keepdims=True)
    xc = sv - mu
    v = xc * lax.rsqrt(jnp.mean(xc * xc, axis=-1, keepdims=True) + EPS)
    v = (v * lng_ref[...] + lnb_ref[...]).astype(BF16)

    lane = lax.broadcasted_iota(jnp.int32, (SG_CHUNK, LANES), 1)
    first = lane < SG_GROUP_DIM
    zero = jnp.zeros((SG_CHUNK, LANES), BF16)
    chunks = []
    for c in range(SUB_ROWS // SG_CHUNK):
        cols = []
        for gp in range(SG_GROUPS // 2):
            v128 = v[c * SG_CHUNK:(c + 1) * SG_CHUNK, gp * LANES:(gp + 1) * LANES]
            rhs = jnp.concatenate([jnp.where(first, v128, zero), jnp.where(first, zero, v128)], axis=0)
            cols.append(jnp.dot(wcat_ref[gp], rhs, preferred_element_type=F32))
        chunks.append(jnp.concatenate(cols, axis=1) + bs_ref[...])
    mixed = jnp.concatenate(chunks, axis=0)
    sgu_ref[rows, :] = _rms(su * mixed, gsg_ref[...]).astype(BF16)


def _row_valid(r, kr, rows):
    rs = min(max(r - NA_KH // 2, 0), rows - NA_KH)
    return 0 <= kr < rows and rs <= kr < rs + NA_KH


def _col_ok():
    qc = np.arange(GRID_W)[:, None]
    kc = np.arange(GRID_W)[None, :]
    col_start = np.clip(qc - NA_KW // 2, 0, GRID_W - NA_KW)
    return (kc >= col_start) & (kc < col_start + NA_KW)


def _minor_offset(part):
    ok = _col_ok()
    own = slice(part * CB_W, (part + 1) * CB_W)
    other = slice((1 - part) * CB_W, (2 - part) * CB_W)
    groups = [g for g in range(CB_W // SUBLANES)
              if ok[part * CB_W + g * SUBLANES:part * CB_W + (g + 1) * SUBLANES, other].any()]
    assert len(groups) == 1 and ok[own, own].any(axis=1).all()
    return groups[0] * SUBLANES


def _bias_plan(rows):
    nblk = rows // QBLK_ROWS
    pieces = {}
    plan = []
    for j in range(nblk):
        kt0 = min(max(j - 1, 0), nblk - BAND_TILES)
        table = []
        for qr in range(QBLK_ROWS):
            r = j * QBLK_ROWS + qr
            for t in range(BAND_TILES):
                kr0 = (kt0 + t) * QBLK_ROWS
                valid = tuple(_row_valid(r, kr0 + kr, rows) for kr in range(QBLK_ROWS))
                for cb in range(COL_HALVES):
                    table.append(pieces.setdefault((kr0 - r, valid, cb), len(pieces)))
        plan.append(table)
    return list(pieces.keys()), np.asarray(plan, np.int32)


def _bias_table(rpb, rows):
    keys, _ = _bias_plan(rows)
    rpb = rpb.astype(F32)
    h, ndr, ndc = rpb.shape
    pad = GRID_W - NA_KW
    ext = jnp.concatenate([jnp.broadcast_to(rpb[..., :1], (h, ndr, pad)), rpb,
                           jnp.broadcast_to(rpb[..., -1:], (h, ndr, pad))], axis=-1)
    toep = jnp.stack([ext[..., GRID_W - 1 - qc:2 * GRID_W - 1 - qc] for qc in range(GRID_W)],
                     axis=2)
    masked = jnp.where(_col_ok(), toep, -jnp.inf)
    dead = jnp.full((h, GRID_W, CB_W), -jnp.inf, F32)
    pieces = []
    for d, valid, cb in keys:
        runs = [masked[:, d + kr + NA_KH - 1, :, cb * CB_W:(cb + 1) * CB_W] if ok else dead
                for kr, ok in enumerate(valid)]
        pieces.append(jnp.concatenate(runs, axis=-1))
    return jnp.stack(pieces, axis=1)


def _attn_kernel(plan_ref, qkv_ref, tab_ref, o_ref, s0_ref, s1_ref, p0_ref, p1_ref, l0_ref, l1_ref,
                 *, rows):
    nblk = rows // QBLK_ROWS
    n_items = HEAD_PAIRS * nblk
    lane = lax.broadcasted_iota(jnp.int32, (QBLK, LANES), 1)
    first = lane < NA_HEAD_DIM

    def coords(i):
        hp = i // nblk
        j = i - hp * nblk
        q0 = pl.multiple_of(j * QBLK, QBLK)
        k0 = pl.multiple_of(jnp.clip(j - 1, 0, nblk - BAND_TILES) * QBLK, QBLK)
        return hp, j, q0, k0

    def scores(i, s_ref):
        hp, _, q0, k0 = coords(i)
        q = qkv_ref[hp, pl.ds(q0, QBLK), :]
        k = qkv_ref[HEAD_PAIRS + hp, pl.ds(k0, BAND), :]
        zero = jnp.zeros_like(q)
        qq = jnp.concatenate([jnp.where(first, q, zero), jnp.where(first, zero, q)], axis=0)
        s_ref[...] = lax.dot_general(qq, k, (((1,), (1,)), ((), ())), preferred_element_type=F32)

    def softmax(i, s_ref, p_ref, l_ref):
        hp, j, _, _ = coords(i)
        n_chunks = 2 * QBLK // SM_ROWS
        halves_per_row = GRID_W // SM_ROWS

        def chunk(r):
            hh, rem = divmod(r, QBLK // SM_ROWS)
            qr, part = divmod(rem, halves_per_row)
            mo = _minor_offset(part)
            r0 = r * SM_ROWS

            def gather(cb, lo, n):
                lts = [COL_HALVES * t + cb for t in range(BAND_TILES)]
                sv = jnp.concatenate(
                    [s_ref[r0 + lo:r0 + lo + n, lt * LANES:(lt + 1) * LANES] for lt in lts], axis=1)
                bias = jnp.concatenate(
                    [tab_ref[2 * hp + hh, plan_ref[(j * QBLK_ROWS + qr) * BAND_PAIRS + lt],
                             part * SM_ROWS + lo:part * SM_ROWS + lo + n, :] for lt in lts], axis=1)
                return sv, bias, lts

            return (r0, mo, functools.partial(gather, part, 0, SM_ROWS),
                    functools.partial(gather, 1 - part, mo, SUBLANES))

        def fold(x, op):
            out = x[:, :LANES]
            for t in range(1, BAND_TILES):
                out = op(out, x[:, t * LANES:(t + 1) * LANES])
            return out

        def merge(main, minor, mo, op):
            parts = [main[:mo], op(main[mo:mo + SUBLANES], minor), main[mo + SUBLANES:]]
            return jnp.concatenate([x for x in parts if x.shape[0]], axis=0)

        maxes = []
        for r in range(n_chunks):
            _, mo, main, minor = chunk(r)
            sv, bias, _ = main()
            m_main = fold(sv + bias, jnp.maximum)
            sv, bias, _ = minor()
            m_minor = fold(sv + bias, jnp.maximum)
            maxes.append(jnp.max(merge(m_main, m_minor, mo, jnp.maximum), axis=-1, keepdims=True))
        for r in range(n_chunks):
            r0, mo, main, minor = chunk(r)
            m = maxes[r]
            sv, bias, lts = main()
            e = jnp.exp(sv + (bias - m))
            l_main = fold(e, jnp.add)
            for n, lt in enumerate(lts):
                p_ref[r0:r0 + SM_ROWS, lt * LANES:(lt + 1) * LANES] = (
                    e[:, n * LANES:(n + 1) * LANES].astype(BF16))
            sv, bias, lts = minor()
            e = jnp.exp(sv + (bias - m[mo:mo + SUBLANES]))
            l_ref[r0:r0 + SM_ROWS, :] = jnp.sum(merge(l_main, fold(e, jnp.add), mo, jnp.add),
                                                axis=-1, keepdims=True)
            pad = jnp.zeros_like(e)
            first_of_pair = mo % (2 * SUBLANES) == 0
            blk = jnp.concatenate([e, pad] if first_of_pair else [pad, e], axis=0).astype(BF16)
            b0 = r0 + mo - (0 if first_of_pair else SUBLANES)
            for n, lt in enumerate(lts):
                p_ref[b0:b0 + 2 * SUBLANES, lt * LANES:(lt + 1) * LANES] = blk[:, n * LANES:(n + 1) * LANES]

    def pv(i, p_ref, l_ref):
        hp, _, q0, k0 = coords(i)
        v = qkv_ref[2 * HEAD_PAIRS + hp, pl.ds(k0, BAND), :]
        o = jnp.dot(p_ref[...], v, preferred_element_type=F32) / l_ref[...]
        o_ref[hp, pl.ds(q0, QBLK), :] = jnp.where(first, o[:QBLK], o[QBLK:]).astype(BF16)

    p0_ref[...] = jnp.zeros_like(p0_ref)
    p1_ref[...] = jnp.zeros_like(p1_ref)
    scores(0, s0_ref)
    scores(1, s1_ref)
    softmax(0, s0_ref, p0_ref, l0_ref)

    def body(t, carry):
        i = 2 * t
        pv(i, p0_ref, l0_ref)
        scores(i + 2, s0_ref)
        softmax(i + 1, s1_ref, p1_ref, l1_ref)
        pv(i + 1, p1_ref, l1_ref)
        scores(i + 3, s1_ref)
        softmax(i + 2, s0_ref, p0_ref, l0_ref)
        return carry

    lax.fori_loop(0, n_items // 2 - 1, body, 0)
    pv(n_items - 2, p0_ref, l0_ref)
    softmax(n_items - 1, s1_ref, p1_ref, l1_ref)
    pv(n_items - 1, p1_ref, l1_ref)


def _ffn_kernel(x_ref, a_ref, s_ref, gna_ref, wout_ref, gpost_ref, gpre2_ref, w1_ref, w2_ref,
                gpost2_ref, o_ref, *, ff_chunk):
    for r0 in range(0, x_ref.shape[0], SUB_ROWS):
        rows = slice(r0, r0 + SUB_ROWS)
        attn = jnp.concatenate([a_ref[hp, rows, :] for hp in range(HEAD_PAIRS)], axis=1).astype(F32)
        y = jnp.dot(_rms(attn, gna_ref[...]).astype(BF16), wout_ref[:NA_WIDTH, :],
                    preferred_element_type=F32)
        y = y + jnp.dot(s_ref[rows, :], wout_ref[NA_WIDTH:, :], preferred_element_type=F32)
        x1 = x_ref[rows, :] + _rms(y, gpost_ref[...])
        h = _rms(x1, gpre2_ref[...]).astype(BF16)
        f = jnp.zeros_like(x1)
        for c in range(D_FF // ff_chunk):
            a = jnp.dot(h, w1_ref[:, c * ff_chunk:(c + 1) * ff_chunk], preferred_element_type=F32)
            a = jnp.square(jnp.maximum(a, 0.0)).astype(BF16)
            f = f + jnp.dot(a, w2_ref[c * ff_chunk:(c + 1) * ff_chunk, :],
                            preferred_element_type=F32)
        o_ref[rows, :] = x1 + _rms(f, gpost2_ref[...])


def _const_spec(shape):
    nd = len(shape)
    return pl.BlockSpec(shape, lambda *_: (0,) * nd, pipeline_mode=pl.Buffered(1))


def _layer(x2, b, s, p):
    t = b * s
    rows = s // GRID_W
    tm_proj = 2 * SUB_ROWS
    tm_ffn = 2 * SUB_ROWS
    row = lambda a: a.reshape(1, -1).astype(F32)

    win = p["w_in"].astype(BF16)
    w_s = p["sg_w_s"].astype(BF16)
    wcat = jnp.concatenate([w_s[0::2], w_s[1::2]], axis=2)
    bs_full = jnp.repeat(p["sg_b_s"].astype(F32).T, SG_GROUP_DIM, axis=1)

    qkv, sgu = pl.pallas_call(
        _proj_kernel,
        grid=(t // tm_proj,),
        in_specs=[
            pl.BlockSpec((tm_proj, D_MODEL), lambda i: (i, 0)),
            _const_spec((1, D_MODEL)),
            _const_spec(win.shape),
            _const_spec((1, SG_WIDTH)),
            _const_spec((1, SG_WIDTH)),
            _const_spec(wcat.shape),
            _const_spec(bs_full.shape),
            _const_spec((1, SG_WIDTH)),
        ],
        out_specs=[
            pl.BlockSpec((3 * HEAD_PAIRS, tm_proj, LANES), lambda i: (0, i, 0)),
            pl.BlockSpec((tm_proj, SG_WIDTH), lambda i: (i, 0)),
        ],
        out_shape=[
            jax.ShapeDtypeStruct((3 * HEAD_PAIRS, t, LANES), BF16),
            jax.ShapeDtypeStruct((t, SG_WIDTH), BF16),
        ],
        compiler_params=pltpu.CompilerParams(
            dimension_semantics=("arbitrary",), vmem_limit_bytes=VMEM_LIMIT),
        name="proj_sgu",
    )(x2, row(p["norm_mix_pre"]), win, row(p["sg_ln_g"]), row(p["sg_ln_b"]), wcat, bs_full,
      row(p["g_out_sg"]))

    tab = _bias_table(p["na_rpb"], rows)
    plan = jnp.asarray(_bias_plan(rows)[1].reshape(-1))
    attn = pl.pallas_call(
        functools.partial(_attn_kernel, rows=rows),
        grid_spec=pltpu.PrefetchScalarGridSpec(
            num_scalar_prefetch=1,
            grid=(b,),
            in_specs=[
                pl.BlockSpec((3 * HEAD_PAIRS, s, LANES), lambda i, _: (0, i, 0)),
                _const_spec(tab.shape),
            ],
            out_specs=pl.BlockSpec((HEAD_PAIRS, s, LANES), lambda i, _: (0, i, 0)),
            scratch_shapes=[
                pltpu.VMEM((2 * QBLK, BAND), F32),
                pltpu.VMEM((2 * QBLK, BAND), F32),
                pltpu.VMEM((2 * QBLK, BAND), BF16),
                pltpu.VMEM((2 * QBLK, BAND), BF16),
                pltpu.VMEM((2 * QBLK, 1), F32),
                pltpu.VMEM((2 * QBLK, 1), F32),
            ],
        ),
        out_shape=jax.ShapeDtypeStruct((HEAD_PAIRS, t, LANES), BF16),
        compiler_params=pltpu.CompilerParams(
            dimension_semantics=("arbitrary",), vmem_limit_bytes=VMEM_LIMIT),
        name="natten",
    )(plan, qkv, tab)

    wout = p["w_out"].astype(BF16)
    w1 = p["w_ff1"].astype(BF16)
    w2 = p["w_ff2"].astype(BF16)
    out = pl.pallas_call(
        functools.partial(_ffn_kernel, ff_chunk=1024),
        grid=(t // tm_ffn,),
        in_specs=[
            pl.BlockSpec((tm_ffn, D_MODEL), lambda i: (i, 0)),
            pl.BlockSpec((HEAD_PAIRS, tm_ffn, LANES), lambda i: (0, i, 0)),
            pl.BlockSpec((tm_ffn, SG_WIDTH), lambda i: (i, 0)),
            _const_spec((1, NA_WIDTH)),
            _const_spec(wout.shape),
            _const_spec((1, D_MODEL)),
            _const_spec((1, D_MODEL)),
            _const_spec(w1.shape),
            _const_spec(w2.shape),
            _const_spec((1, D_MODEL)),
        ],
        out_specs=pl.BlockSpec((tm_ffn, D_MODEL), lambda i: (i, 0)),
        out_shape=jax.ShapeDtypeStruct((t, D_MODEL), F32),
        compiler_params=pltpu.CompilerParams(
            dimension_semantics=("arbitrary",), vmem_limit_bytes=VMEM_LIMIT),
        name="outproj_ffn",
    )(x2, attn, sgu, row(p["g_out_na"]), wout, row(p["norm_mix_post"]), row(p["norm_ffn_pre"]),
      w1, w2, row(p["norm_ffn_post"]))
    return out


def kernel(x, norm_mix_pre, w_in, na_rpb, sg_ln_g, sg_ln_b, sg_w_s, sg_b_s, g_out_na, g_out_sg,
           w_out, norm_mix_post, norm_ffn_pre, w_ff1, w_ff2, norm_ffn_post):
    b, s, d = x.shape
    assert d == D_MODEL and s % (QBLK * 2) == 0 and s // GRID_W >= 2 * NA_KH
    params = dict(norm_mix_pre=norm_mix_pre, w_in=w_in, na_rpb=na_rpb, sg_ln_g=sg_ln_g,
                  sg_ln_b=sg_ln_b, sg_w_s=sg_w_s, sg_b_s=sg_b_s, g_out_na=g_out_na,
                  g_out_sg=g_out_sg, w_out=w_out, norm_mix_post=norm_mix_post,
                  norm_ffn_pre=norm_ffn_pre, w_ff1=w_ff1, w_ff2=w_ff2, norm_ffn_post=norm_ffn_post)
    x2 = x.reshape(b * s, d)
    for l in range(norm_mix_pre.shape[0]):
        x2 = _layer(x2, b, s, {k: v[l] for k, v in params.items()})
    return x2.reshape(b, s, d)
```

```python
import functools

import numpy as np
import jax
import jax.numpy as jnp
from jax import lax
from jax.experimental import pallas as pl
from jax.experimental.pallas import tpu as pltpu

D_MODEL = 1024
GRID_W = 64
NA_HEADS = 8
NA_HEAD_DIM = 64
NA_WIDTH = NA_HEADS * NA_HEAD_DIM
NA_KH = 8
NA_KW = 16
SG_GROUPS = 8
SG_GROUP_DIM = 64
SG_WIDTH = SG_GROUPS * SG_GROUP_DIM
SG_CHUNK = 128
D_FF = 4 * D_MODEL
EPS = 1e-6

LANES = 128
HEAD_PAIRS = NA_HEADS // 2
QBLK_ROWS = 4
QBLK = QBLK_ROWS * GRID_W
BAND_TILES = 3
BAND = BAND_TILES * QBLK
BAND_PAIRS = BAND // LANES
SM_ROWS = 32
COL_HALVES = 2
CB_W = GRID_W // COL_HALVES
SUBLANES = 8
_KEY_RUNS = tuple((kr * GRID_W + cb * CB_W, cb * LANES + kr * CB_W)
                  for kr in range(QBLK_ROWS) for cb in range(COL_HALVES))
SUB_ROWS = 512
VMEM_LIMIT = 56 * 1024 * 1024

F32 = jnp.float32
BF16 = jnp.bfloat16


def _rms(x, g):
    return x * lax.rsqrt(jnp.mean(x * x, axis=-1, keepdims=True) + EPS) * g


def _proj_kernel(x_ref, gpre_ref, win_ref, lng_ref, lnb_ref, wcat_ref, bs_ref, gsg_ref,
                 qkv_ref, sgu_ref):
    for r0 in range(0, x_ref.shape[0], SUB_ROWS):
        _proj_rows(r0, x_ref, gpre_ref, win_ref, lng_ref, lnb_ref, wcat_ref, bs_ref, gsg_ref,
                   qkv_ref, sgu_ref)


def _proj_rows(r0, x_ref, gpre_ref, win_ref, lng_ref, lnb_ref, wcat_ref, bs_ref, gsg_ref,
               qkv_ref, sgu_ref):
    rows = slice(r0, r0 + SUB_ROWS)
    h = _rms(x_ref[rows, :], gpre_ref[...]).astype(BF16)

    def proj(c0, width):
        return jnp.dot(h, win_ref[:, c0:c0 + width], preferred_element_type=F32)

    scale = NA_HEAD_DIM ** -0.5
    for part in range(3):
        p = proj(part * NA_WIDTH, NA_WIDTH)
        if part == 0:
            p = p * scale
        for hp in range(HEAD_PAIRS):
            cols = p[:, hp * LANES:(hp + 1) * LANES].astype(BF16)
            if part == 0:
                qkv_ref[hp, rows, :] = cols
                continue
            for src, dst in _KEY_RUNS:
                for t0 in range(0, SUB_ROWS, QBLK):
                    qkv_ref[part * HEAD_PAIRS + hp, pl.ds(r0 + t0 + dst, CB_W), :] = (
                        cols[t0 + src:t0 + src + CB_W, :])

    su = jax.nn.gelu(proj(3 * NA_WIDTH, SG_WIDTH))
    sv = jax.nn.gelu(proj(3 * NA_WIDTH + SG_WIDTH, SG_WIDTH))
    mu = jnp.mean(sv, axis=-1, keepdims=True)
    xc = sv - mu
    v = xc * lax.rsqrt(jnp.mean(xc * xc, axis=-1, keepdims=True) + EPS)
    v = (v * lng_ref[...] + lnb_ref[...]).astype(BF16)

    lane = lax.broadcasted_iota(jnp.int32, (SG_CHUNK, LANES), 1)
    first = lane < SG_GROUP_DIM
    zero = jnp.zeros((SG_CHUNK, LANES), BF16)
    chunks = []
    for c in range(SUB_ROWS // SG_CHUNK):
        cols = []
        for gp in range(SG_GROUPS // 2):
            v128 = v[c * SG_CHUNK:(c + 1) * SG_CHUNK, gp * LANES:(gp + 1) * LANES]
            rhs = jnp.concatenate([jnp.where(first, v128, zero), jnp.where(first, zero, v128)], axis=0)
            cols.append(jnp.dot(wcat_ref[gp], rhs, preferred_element_type=F32))
        chunks.append(jnp.concatenate(cols, axis=1) + bs_ref[...])
    mixed = jnp.concatenate(chunks, axis=0)
    sgu_ref[rows, :] = _rms(su * mixed, gsg_ref[...]).astype(BF16)


def _row_valid(r, kr, rows):
    rs = min(max(r - NA_KH // 2, 0), rows - NA_KH)
    return 0 <= kr < rows and rs <= kr < rs + NA_KH


def _col_ok():
    qc = np.arange(GRID_W)[:, None]
    kc = np.arange(GRID_W)[None, :]
    col_start = np.clip(qc - NA_KW // 2, 0, GRID_W - NA_KW)
    return (kc >= col_start) & (kc < col_start + NA_KW)


def _minor_offset(part):
    ok = _col_ok()
    own = slice(part * CB_W, (part + 1) * CB_W)
    other = slice((1 - part) * CB_W, (2 - part) * CB_W)
    groups = [g for g in range(CB_W // SUBLANES)
              if ok[part * CB_W + g * SUBLANES:part * CB_W + (g + 1) * SUBLANES, other].any()]
    assert len(groups) == 1 and ok[own, own].any(axis=1).all()
    return groups[0] * SUBLANES


def _bias_plan(rows):
    nblk = rows // QBLK_ROWS
    pieces = {}
    plan = []
    for j in range(nblk):
        kt0 = min(max(j - 1, 0), nblk - BAND_TILES)
        table = []
        for qr in range(QBLK_ROWS):
            r = j * QBLK_ROWS + qr
            for t in range(BAND_TILES):
                kr0 = (kt0 + t) * QBLK_ROWS
                valid = tuple(_row_valid(r, kr0 + kr, rows) for kr in range(QBLK_ROWS))
                for cb in range(COL_HALVES):
                    table.append(pieces.setdefault((kr0 - r, valid, cb), len(pieces)))
        plan.append(table)
    return list(pieces.keys()), np.asarray(plan, np.int32)


def _bias_table(rpb, rows):
    keys, _ = _bias_plan(rows)
    rpb = rpb.astype(F32)
    h, ndr, ndc = rpb.shape
    qc = np.arange(GRID_W)[:, None]
    kc = np.arange(GRID_W)[None, :]
    dc = np.clip(kc - qc, -(NA_KW - 1), NA_KW - 1) + NA_KW - 1
    onehot = (dc[None] == np.arange(ndc)[:, None, None]).astype(np.float32)
    onehot = onehot.reshape(ndc, GRID_W, COL_HALVES, CB_W).transpose(0, 2, 1, 3)
    toep = jnp.einsum("hrd,dbqc->hrbqc", rpb, onehot, precision=lax.Precision.HIGHEST)
    sel = np.zeros((len(keys), QBLK_ROWS, ndr, COL_HALVES), np.float32)
    live = np.zeros((len(keys), GRID_W, QBLK_ROWS, CB_W), bool)
    col_ok = _col_ok()
    for p, (d, valid, cb) in enumerate(keys):
        for kr, ok in enumerate(valid):
            if ok:
                sel[p, kr, d + kr + NA_KH - 1, cb] = 1.0
                live[p, :, kr, :] = col_ok[:, cb * CB_W:(cb + 1) * CB_W]
    tab = jnp.einsum("pkrb,hrbqc->hpqkc", sel, toep, precision=lax.Precision.HIGHEST)
    return jnp.where(live, tab, -jnp.inf).reshape(h, len(keys), GRID_W, LANES)


def _attn_kernel(plan_ref, qkv_ref, tab_ref, o_ref, s0_ref, s1_ref, p0_ref, p1_ref, l0_ref, l1_ref,
                 *, rows):
    nblk = rows // QBLK_ROWS
    n_items = HEAD_PAIRS * nblk
    lane = lax.broadcasted_iota(jnp.int32, (QBLK, LANES), 1)
    first = lane < NA_HEAD_DIM

    def coords(i):
        hp = i // nblk
        j = i - hp * nblk
        q0 = pl.multiple_of(j * QBLK, QBLK)
        k0 = pl.multiple_of(jnp.clip(j - 1, 0, nblk - BAND_TILES) * QBLK, QBLK)
        return hp, j, q0, k0

    def scores(i, s_ref):
        hp, _, q0, k0 = coords(i)
        q = qkv_ref[hp, pl.ds(q0, QBLK), :]
        k = qkv_ref[HEAD_PAIRS + hp, pl.ds(k0, BAND), :]
        zero = jnp.zeros_like(q)
        qq = jnp.concatenate([jnp.where(first, q, zero), jnp.where(first, zero, q)], axis=0)
        s_ref[...] = lax.dot_general(qq, k, (((1,), (1,)), ((), ())), preferred_element_type=F32)

    def softmax(i, s_ref, p_ref, l_ref):
        hp, j, _, _ = coords(i)
        n_chunks = 2 * QBLK // SM_ROWS
        halves_per_row = GRID_W // SM_ROWS

        def chunk(r):
            hh, rem = divmod(r, QBLK // SM_ROWS)
            qr, part = divmod(rem, halves_per_row)
            mo = _minor_offset(part)
            r0 = r * SM_ROWS

            def gather(cb, lo, n):
                lts = [COL_HALVES * t + cb for t in range(BAND_TILES)]
                sv = jnp.concatenate(
                    [s_ref[r0 + lo:r0 + lo + n, lt * LANES:(lt + 1) * LANES] for lt in lts], axis=1)
                bias = jnp.concatenate(
                    [tab_ref[2 * hp + hh, plan_ref[(j * QBLK_ROWS + qr) * BAND_PAIRS + lt],
                             part * SM_ROWS + lo:part * SM_ROWS + lo + n, :] for lt in lts], axis=1)
                return sv, bias, lts

            return (r0, mo, functools.partial(gather, part, 0, SM_ROWS),
                    functools.partial(gather, 1 - part, mo, SUBLANES))

        def fold(x, op):
            out = x[:, :LANES]
            for t in range(1, BAND_TILES):
                out = op(out, x[:, t * LANES:(t + 1) * LANES])
            return out

        def merge(main, minor, mo, op):
            parts = [main[:mo], op(main[mo:mo + SUBLANES], minor), main[mo + SUBLANES:]]
            return jnp.concatenate([x for x in parts if x.shape[0]], axis=0)

        maxes = []
        for r in range(n_chunks):
            _, mo, main, minor = chunk(r)
            sv, bias, _ = main()
            m_main = fold(sv + bias, jnp.maximum)
            sv, bias, _ = minor()
            m_minor = fold(sv + bias, jnp.maximum)
            maxes.append(jnp.max(merge(m_main, m_minor, mo, jnp.maximum), axis=-1, keepdims=True))
        for r in range(n_chunks):
            r0, mo, main, minor = chunk(r)
            m = maxes[r]
            sv, bias, lts = main()
            e = jnp.exp(sv + (bias - m))
            l_main = fold(e, jnp.add)
            for n, lt in enumerate(lts):
                p_ref[r0:r0 + SM_ROWS, lt * LANES:(lt + 1) * LANES] = (
                    e[:, n * LANES:(n + 1) * LANES].astype(BF16))
            sv, bias, lts = minor()
            e = jnp.exp(sv + (bias - m[mo:mo + SUBLANES]))
            l_ref[r0:r0 + SM_ROWS, :] = jnp.sum(merge(l_main, fold(e, jnp.add), mo, jnp.add),
                                                axis=-1, keepdims=True)
            pad = jnp.zeros_like(e)
            first_of_pair = mo % (2 * SUBLANES) == 0
            blk = jnp.concatenate([e, pad] if first_of_pair else [pad, e], axis=0).astype(BF16)
            b0 = r0 + mo - (0 if first_of_pair else SUBLANES)
            for n, lt in enumerate(lts):
                p_ref[b0:b0 + 2 * SUBLANES, lt * LANES:(lt + 1) * LANES] = blk[:, n * LANES:(n + 1) * LANES]

    def pv(i, p_ref, l_ref):
        hp, _, q0, k0 = coords(i)
        v = qkv_ref[2 * HEAD_PAIRS + hp, pl.ds(k0, BAND), :]
        o = jnp.dot(p_ref[...], v, preferred_element_type=F32) / l_ref[...]
        o_ref[hp, pl.ds(q0, QBLK), :] = jnp.where(first, o[:QBLK], o[QBLK:]).astype(BF16)

    p0_ref[...] = jnp.zeros_like(p0_ref)
    p1_ref[...] = jnp.zeros_like(p1_ref)
    scores(0, s0_ref)
    scores(1, s1_ref)
    softmax(0, s0_ref, p0_ref, l0_ref)

    def body(t, carry):
        i = 2 * t
        pv(i, p0_ref, l0_ref)
        scores(i + 2, s0_ref)
        softmax(i + 1, s1_ref, p1_ref, l1_ref)
        pv(i + 1, p1_ref, l1_ref)
        scores(i + 3, s1_ref)
        softmax(i + 2, s0_ref, p0_ref, l0_ref)
        return carry

    lax.fori_loop(0, n_items // 2 - 1, body, 0)
    pv(n_items - 2, p0_ref, l0_ref)
    softmax(n_items - 1, s1_ref, p1_ref, l1_ref)
    pv(n_items - 1, p1_ref, l1_ref)


def _ffn_kernel(x_ref, a_ref, s_ref, gna_ref, wout_ref, gpost_ref, gpre2_ref, w1_ref, w2_ref,
                gpost2_ref, o_ref, *, ff_chunk):
    for r0 in range(0, x_ref.shape[0], SUB_ROWS):
        rows = slice(r0, r0 + SUB_ROWS)
        attn = jnp.concatenate([a_ref[hp, rows, :] for hp in range(HEAD_PAIRS)], axis=1).astype(F32)
        y = jnp.dot(_rms(attn, gna_ref[...]).astype(BF16), wout_ref[:NA_WIDTH, :],
                    preferred_element_type=F32)
        y = y + jnp.dot(s_ref[rows, :], wout_ref[NA_WIDTH:, :], preferred_element_type=F32)
        x1 = x_ref[rows, :] + _rms(y, gpost_ref[...])
        h = _rms(x1, gpre2_ref[...]).astype(BF16)
        f = jnp.zeros_like(x1)
        for c in range(D_FF // ff_chunk):
            a = jnp.dot(h, w1_ref[:, c * ff_chunk:(c + 1) * ff_chunk], preferred_element_type=F32)
            a = jnp.square(jnp.maximum(a, 0.0)).astype(BF16)
            f = f + jnp.dot(a, w2_ref[c * ff_chunk:(c + 1) * ff_chunk, :],
                            preferred_element_type=F32)
        o_ref[rows, :] = x1 + _rms(f, gpost2_ref[...])


def _const_spec(shape):
    nd = len(shape)
    return pl.BlockSpec(shape, lambda *_: (0,) * nd, pipeline_mode=pl.Buffered(1))


def _layer(x2, b, s, p):
    t = b * s
    rows = s // GRID_W
    tm_proj = 2 * SUB_ROWS
    tm_ffn = 2 * SUB_ROWS
    row = lambda a: a.reshape(1, -1).astype(F32)

    win = p["w_in"].astype(BF16)
    w_s = p["sg_w_s"].astype(BF16)
    wcat = jnp.concatenate([w_s[0::2], w_s[1::2]], axis=2)
    bs_full = jnp.repeat(p["sg_b_s"].astype(F32).T, SG_GROUP_DIM, axis=1)

    qkv, sgu = pl.pallas_call(
        _proj_kernel,
        grid=(t // tm_proj,),
        in_specs=[
            pl.BlockSpec((tm_proj, D_MODEL), lambda i: (i, 0)),
            _const_spec((1, D_MODEL)),
            _const_spec(win.shape),
            _const_spec((1, SG_WIDTH)),
            _const_spec((1, SG_WIDTH)),
            _const_spec(wcat.shape),
            _const_spec(bs_full.shape),
            _const_spec((1, SG_WIDTH)),
        ],
        out_specs=[
            pl.BlockSpec((3 * HEAD_PAIRS, tm_proj, LANES), lambda i: (0, i, 0)),
            pl.BlockSpec((tm_proj, SG_WIDTH), lambda i: (i, 0)),
        ],
        out_shape=[
            jax.ShapeDtypeStruct((3 * HEAD_PAIRS, t, LANES), BF16),
            jax.ShapeDtypeStruct((t, SG_WIDTH), BF16),
        ],
        compiler_params=pltpu.CompilerParams(
            dimension_semantics=("arbitrary",), vmem_limit_bytes=VMEM_LIMIT),
        name="proj_sgu",
    )(x2, row(p["norm_mix_pre"]), win, row(p["sg_ln_g"]), row(p["sg_ln_b"]), wcat, bs_full,
      row(p["g_out_sg"]))

    tab = _bias_table(p["na_rpb"], rows)
    plan = jnp.asarray(_bias_plan(rows)[1].reshape(-1))
    attn = pl.pallas_call(
        functools.partial(_attn_kernel, rows=rows),
        grid_spec=pltpu.PrefetchScalarGridSpec(
            num_scalar_prefetch=1,
            grid=(b,),
            in_specs=[
                pl.BlockSpec((3 * HEAD_PAIRS, s, LANES), lambda i, _: (0, i, 0)),
                _const_spec(tab.shape),
            ],
            out_specs=pl.BlockSpec((HEAD_PAIRS, s, LANES), lambda i, _: (0, i, 0)),
            scratch_shapes=[
                pltpu.VMEM((2 * QBLK, BAND), F32),
                pltpu.VMEM((2 * QBLK, BAND), F32),
                pltpu.VMEM((2 * QBLK, BAND), BF16),
                pltpu.VMEM((2 * QBLK, BAND), BF16),
                pltpu.VMEM((2 * QBLK, 1), F32),
                pltpu.VMEM((2 * QBLK, 1), F32),
            ],
        ),
        out_shape=jax.ShapeDtypeStruct((HEAD_PAIRS, t, LANES), BF16),
        compiler_params=pltpu.CompilerParams(
            dimension_semantics=("arbitrary",), vmem_limit_bytes=VMEM_LIMIT),
        name="natten",
    )(plan, qkv, tab)

    wout = p["w_out"].astype(BF16)
    w1 = p["w_ff1"].astype(BF16)
    w2 = p["w_ff2"].astype(BF16)
    out = pl.pallas_call(
        functools.partial(_ffn_kernel, ff_chunk=1024),
        grid=(t // tm_ffn,),
        in_specs=[
            pl.BlockSpec((tm_ffn, D_MODEL), lambda i: (i, 0)),
            pl.BlockSpec((HEAD_PAIRS, tm_ffn, LANES), lambda i: (0, i, 0)),
            pl.BlockSpec((tm_ffn, SG_WIDTH), lambda i: (i, 0)),
            _const_spec((1, NA_WIDTH)),
            _const_spec(wout.shape),
            _const_spec((1, D_MODEL)),
            _const_spec((1, D_MODEL)),
            _const_spec(w1.shape),
            _const_spec(w2.shape),
            _const_spec((1, D_MODEL)),
        ],
        out_specs=pl.BlockSpec((tm_ffn, D_MODEL), lambda i: (i, 0)),
        out_shape=jax.ShapeDtypeStruct((t, D_MODEL), F32),
        compiler_params=pltpu.CompilerParams(
            dimension_semantics=("arbitrary",), vmem_limit_bytes=VMEM_LIMIT),
        name="outproj_ffn",
    )(x2, attn, sgu, row(p["g_out_na"]), wout, row(p["norm_mix_post"]), row(p["norm_ffn_pre"]),
      w1, w2, row(p["norm_ffn_post"]))
    return out


def kernel(x, norm_mix_pre, w_in, na_rpb, sg_ln_g, sg_ln_b, sg_w_s, sg_b_s, g_out_na, g_out_sg,
           w_out, norm_mix_post, norm_ffn_pre, w_ff1, w_ff2, norm_ffn_post):
    b, s, d = x.shape
    assert d == D_MODEL and s % (QBLK * 2) == 0 and s // GRID_W >= 2 * NA_KH
    params = dict(norm_mix_pre=norm_mix_pre, w_in=w_in, na_rpb=na_rpb, sg_ln_g=sg_ln_g,
                  sg_ln_b=sg_ln_b, sg_w_s=sg_w_s, sg_b_s=sg_b_s, g_out_na=g_out_na,
                  g_out_sg=g_out_sg, w_out=w_out, norm_mix_post=norm_mix_post,
                  norm_ffn_pre=norm_ffn_pre, w_ff1=w_ff1, w_ff2=w_ff2, norm_ffn_post=norm_ffn_post)
    x2 = x.reshape(b * s, d)
    for l in range(norm_mix_pre.shape[0]):
        x2 = _layer(x2, b, s, {k: v[l] for k, v in params.items()})
    return x2.reshape(b, s, d)
```

```python
import functools

import numpy as np
import jax
import jax.numpy as jnp
from jax import lax
from jax.experimental import pallas as pl
from jax.experimental.pallas import tpu as pltpu

D_MODEL = 1024
GRID_W = 64
NA_HEADS = 8
NA_HEAD_DIM = 64
NA_WIDTH = NA_HEADS * NA_HEAD_DIM
NA_KH = 8
NA_KW = 16
SG_GROUPS = 8
SG_GROUP_DIM = 64
SG_WIDTH = SG_GROUPS * SG_GROUP_DIM
SG_CHUNK = 128
D_FF = 4 * D_MODEL
EPS = 1e-6

LANES = 128
HEAD_PAIRS = NA_HEADS // 2
QBLK_ROWS = 4
QBLK = QBLK_ROWS * GRID_W
BAND_TILES = 3
BAND = BAND_TILES * QBLK
BAND_PAIRS = BAND // LANES
SM_ROWS = 32
COL_HALVES = 2
CB_W = GRID_W // COL_HALVES
SUBLANES = 8
_KEY_RUNS = tuple((kr * GRID_W + cb * CB_W, cb * LANES + kr * CB_W)
                  for kr in range(QBLK_ROWS) for cb in range(COL_HALVES))
SUB_ROWS = 512
VMEM_LIMIT = 56 * 1024 * 1024

F32 = jnp.float32
BF16 = jnp.bfloat16


def _rms(x, g):
    return x * lax.rsqrt(jnp.mean(x * x, axis=-1, keepdims=True) + EPS) * g


def _proj_kernel(x_ref, gpre_ref, win_ref, lng_ref, lnb_ref, wcat_ref, bs_ref, gsg_ref,
                 qkv_ref, sgu_ref):
    for r0 in range(0, x_ref.shape[0], SUB_ROWS):
        _proj_rows(r0, x_ref, gpre_ref, win_ref, lng_ref, lnb_ref, wcat_ref, bs_ref, gsg_ref,
                   qkv_ref, sgu_ref)


def _proj_rows(r0, x_ref, gpre_ref, win_ref, lng_ref, lnb_ref, wcat_ref, bs_ref, gsg_ref,
               qkv_ref, sgu_ref):
    rows = slice(r0, r0 + SUB_ROWS)
    h = _rms(x_ref[rows, :], gpre_ref[...]).astype(BF16)

    def proj(c0, width):
        return jnp.dot(h, win_ref[:, c0:c0 + width], preferred_element_type=F32)

    scale = NA_HEAD_DIM ** -0.5
    for part in range(3):
        p = proj(part * NA_WIDTH, NA_WIDTH)
        if part == 0:
            p = p * scale
        for hp in range(HEAD_PAIRS):
            cols = p[:, hp * LANES:(hp + 1) * LANES].astype(BF16)
            if part == 0:
                qkv_ref[hp, rows, :] = cols
                continue
            for src, dst in _KEY_RUNS:
                for t0 in range(0, SUB_ROWS, QBLK):
                    qkv_ref[part * HEAD_PAIRS + hp, pl.ds(r0 + t0 + dst, CB_W), :] = (
                        cols[t0 + src:t0 + src + CB_W, :])

    su = jax.nn.gelu(proj(3 * NA_WIDTH, SG_WIDTH))
    sv = jax.nn.gelu(proj(3 * NA_WIDTH + SG_WIDTH, SG_WIDTH))
    mu = jnp.mean(sv, axis=-1, keepdims=True)
    xc = sv - mu
    v = xc * lax.rsqrt(jnp.mean(xc * xc, axis=-1, keepdims=True) + EPS)
    v = (v * lng_ref[...] + lnb_ref[...]).astype(BF16)

    lane = lax.broadcasted_iota(jnp.int32, (SG_CHUNK, LANES), 1)
    first = lane < SG_GROUP_DIM
    zero = jnp.zeros((SG_CHUNK, LANES), BF16)
    chunks = []
    for c in range(SUB_ROWS // SG_CHUNK):
        cols = []
        for gp in range(SG_GROUPS // 2):
            v128 = v[c * SG_CHUNK:(c + 1) * SG_CHUNK, gp * LANES:(gp + 1) * LANES]
            rhs = jnp.concatenate([jnp.where(first, v128, zero), jnp.where(first, zero, v128)], axis=0)
            cols.append(jnp.dot(wcat_ref[gp], rhs, preferred_element_type=F32))
        chunks.append(jnp.concatenate(cols, axis=1) + bs_ref[...])
    mixed = jnp.concatenate(chunks, axis=0)
    sgu_ref[rows, :] = _rms(su * mixed, gsg_ref[...]).astype(BF16)


def _row_valid(r, kr, rows):
    rs = min(max(r - NA_KH // 2, 0), rows - NA_KH)
    return 0 <= kr < rows and rs <= kr < rs + NA_KH


def _col_ok():
    qc = np.arange(GRID_W)[:, None]
    kc = np.arange(GRID_W)[None, :]
    col_start = np.clip(qc - NA_KW // 2, 0, GRID_W - NA_KW)
    return (kc >= col_start) & (kc < col_start + NA_KW)


def _minor_offset(part):
    ok = _col_ok()
    own = slice(part * CB_W, (part + 1) * CB_W)
    other = slice((1 - part) * CB_W, (2 - part) * CB_W)
    groups = [g for g in range(CB_W // SUBLANES)
              if ok[part * CB_W + g * SUBLANES:part * CB_W + (g + 1) * SUBLANES, other].any()]
    assert len(groups) == 1 and ok[own, own].any(axis=1).all()
    return groups[0] * SUBLANES


def _bias_plan(rows):
    nblk = rows // QBLK_ROWS
    pieces = {}
    plan = []
    for j in range(nblk):
        kt0 = min(max(j - 1, 0), nblk - BAND_TILES)
        table = []
        for qr in range(QBLK_ROWS):
            r = j * QBLK_ROWS + qr
            for t in range(BAND_TILES):
                kr0 = (kt0 + t) * QBLK_ROWS
                valid = tuple(_row_valid(r, kr0 + kr, rows) for kr in range(QBLK_ROWS))
                for cb in range(COL_HALVES):
                    table.append(pieces.setdefault((kr0 - r, valid, cb), len(pieces)))
        plan.append(table)
    return list(pieces.keys()), np.asarray(plan, np.int32)


def _bias_table(rpb, rows):
    keys, _ = _bias_plan(rows)
    rpb = rpb.astype(F32)
    h, ndr, ndc = rpb.shape
    qc = np.arange(GRID_W)[:, None]
    kc = np.arange(GRID_W)[None, :]
    dc = np.clip(kc - qc, -(NA_KW - 1), NA_KW - 1) + NA_KW - 1
    onehot = (dc[None] == np.arange(ndc)[:, None, None]).astype(np.float32)
    onehot = onehot.reshape(ndc, GRID_W, COL_HALVES, CB_W).transpose(0, 2, 1, 3)
    hi = lax.Precision.HIGHEST
    toep = jnp.einsum("hrd,dbqc->hrbqc", rpb, onehot, precision=hi)
    sel = np.zeros((QBLK_ROWS, h, len(keys), ndr, COL_HALVES), np.float32)
    live = np.zeros((len(keys), GRID_W, QBLK_ROWS, CB_W), bool)
    col_ok = _col_ok()
    for p, (d, valid, cb) in enumerate(keys):
        for kr, ok in enumerate(valid):
            if ok:
                sel[kr, :, p, d + kr + NA_KH - 1, cb] = 1.0
                live[p, :, kr, :] = col_ok[:, cb * CB_W:(cb + 1) * CB_W]
    runs = [jnp.einsum("hprb,hrbqc->hpqc", sel[kr], toep, precision=hi) for kr in range(QBLK_ROWS)]
    tab = jnp.concatenate(runs, axis=-1)
    return jnp.where(live.reshape(len(keys), GRID_W, LANES), tab, -jnp.inf)


def _attn_kernel(plan_ref, qkv_ref, tab_ref, o_ref, s0_ref, s1_ref, p0_ref, p1_ref, l0_ref, l1_ref,
                 *, rows):
    nblk = rows // QBLK_ROWS
    n_items = HEAD_PAIRS * nblk
    lane = lax.broadcasted_iota(jnp.int32, (QBLK, LANES), 1)
    first = lane < NA_HEAD_DIM

    def coords(i):
        hp = i // nblk
        j = i - hp * nblk
        q0 = pl.multiple_of(j * QBLK, QBLK)
        k0 = pl.multiple_of(jnp.clip(j - 1, 0, nblk - BAND_TILES) * QBLK, QBLK)
        return hp, j, q0, k0

    def scores(i, s_ref):
        hp, _, q0, k0 = coords(i)
        q = qkv_ref[hp, pl.ds(q0, QBLK), :]
        k = qkv_ref[HEAD_PAIRS + hp, pl.ds(k0, BAND), :]
        zero = jnp.zeros_like(q)
        qq = jnp.concatenate([jnp.where(first, q, zero), jnp.where(first, zero, q)], axis=0)
        s_ref[...] = lax.dot_general(qq, k, (((1,), (1,)), ((), ())), preferred_element_type=F32)

    def softmax(i, s_ref, p_ref, l_ref):
        hp, j, _, _ = coords(i)
        n_chunks = 2 * QBLK // SM_ROWS
        halves_per_row = GRID_W // SM_ROWS

        def chunk(r):
            hh, rem = divmod(r, QBLK // SM_ROWS)
            qr, part = divmod(rem, halves_per_row)
            mo = _minor_offset(part)
            r0 = r * SM_ROWS

            def gather(cb, lo, n):
                lts = [COL_HALVES * t + cb for t in range(BAND_TILES)]
                sv = jnp.concatenate(
                    [s_ref[r0 + lo:r0 + lo + n, lt * LANES:(lt + 1) * LANES] for lt in lts], axis=1)
                bias = jnp.concatenate(
                    [tab_ref[2 * hp + hh, plan_ref[(j * QBLK_ROWS + qr) * BAND_PAIRS + lt],
                             part * SM_ROWS + lo:part * SM_ROWS + lo + n, :] for lt in lts], axis=1)
                return sv, bias, lts

            return (r0, mo, functools.partial(gather, part, 0, SM_ROWS),
                    functools.partial(gather, 1 - part, mo, SUBLANES))

        def fold(x, op):
            out = x[:, :LANES]
            for t in range(1, BAND_TILES):
                out = op(out, x[:, t * LANES:(t + 1) * LANES])
            return out

        def merge(main, minor, mo, op):
            parts = [main[:mo], op(main[mo:mo + SUBLANES], minor), main[mo + SUBLANES:]]
            return jnp.concatenate([x for x in parts if x.shape[0]], axis=0)

        maxes = []
        for r in range(n_chunks):
            _, mo, main, minor = chunk(r)
            sv, bias, _ = main()
            m_main = fold(sv + bias, jnp.maximum)
            sv, bias, _ = minor()
            m_minor = fold(sv + bias, jnp.maximum)
            maxes.append(jnp.max(merge(m_main, m_minor, mo, jnp.maximum), axis=-1, keepdims=True))
        for r in range(n_chunks):
            r0, mo, main, minor = chunk(r)
            m = maxes[r]
            sv, bias, lts = main()
            e = jnp.exp(sv + (bias - m))
            l_main = fold(e, jnp.add)
            for n, lt in enumerate(lts):
                p_ref[r0:r0 + SM_ROWS, lt * LANES:(lt + 1) * LANES] = (
                    e[:, n * LANES:(n + 1) * LANES].astype(BF16))
            sv, bias, lts = minor()
            e = jnp.exp(sv + (bias - m[mo:mo + SUBLANES]))
            l_ref[r0:r0 + SM_ROWS, :] = jnp.sum(merge(l_main, fold(e, jnp.add), mo, jnp.add),
                                                axis=-1, keepdims=True)
            pad = jnp.zeros_like(e)
            first_of_pair = mo % (2 * SUBLANES) == 0
            blk = jnp.concatenate([e, pad] if first_of_pair else [pad, e], axis=0).astype(BF16)
            b0 = r0 + mo - (0 if first_of_pair else SUBLANES)
            for n, lt in enumerate(lts):
                p_ref[b0:b0 + 2 * SUBLANES, lt * LANES:(lt + 1) * LANES] = blk[:, n * LANES:(n + 1) * LANES]

    def pv(i, p_ref, l_ref):
        hp, _, q0, k0 = coords(i)
        v = qkv_ref[2 * HEAD_PAIRS + hp, pl.ds(k0, BAND), :]
        o = jnp.dot(p_ref[...], v, preferred_element_type=F32) / l_ref[...]
        o_ref[hp, pl.ds(q0, QBLK), :] = jnp.where(first, o[:QBLK], o[QBLK:]).astype(BF16)

    p0_ref[...] = jnp.zeros_like(p0_ref)
    p1_ref[...] = jnp.zeros_like(p1_ref)
    scores(0, s0_ref)
    scores(1, s1_ref)
    softmax(0, s0_ref, p0_ref, l0_ref)

    def body(t, carry):
        i = 2 * t
        pv(i, p0_ref, l0_ref)
        scores(i + 2, s0_ref)
        softmax(i + 1, s1_ref, p1_ref, l1_ref)
        pv(i + 1, p1_ref, l1_ref)
        scores(i + 3, s1_ref)
        softmax(i + 2, s0_ref, p0_ref, l0_ref)
        return carry

    lax.fori_loop(0, n_items // 2 - 1, body, 0)
    pv(n_items - 2, p0_ref, l0_ref)
    softmax(n_items - 1, s1_ref, p1_ref, l1_ref)
    pv(n_items - 1, p1_ref, l1_ref)


def _ffn_kernel(x_ref, a_ref, s_ref, gna_ref, wout_ref, gpost_ref, gpre2_ref, w1_ref, w2_ref,
                gpost2_ref, o_ref, *, ff_chunk):
    for r0 in range(0, x_ref.shape[0], SUB_ROWS):
        rows = slice(r0, r0 + SUB_ROWS)
        attn = jnp.concatenate([a_ref[hp, rows, :] for hp in range(HEAD_PAIRS)], axis=1).astype(F32)
        y = jnp.dot(_rms(attn, gna_ref[...]).astype(BF16), wout_ref[:NA_WIDTH, :],
                    preferred_element_type=F32)
        y = y + jnp.dot(s_ref[rows, :], wout_ref[NA_WIDTH:, :], preferred_element_type=F32)
        x1 = x_ref[rows, :] + _rms(y, gpost_ref[...])
        h = _rms(x1, gpre2_ref[...]).astype(BF16)
        f = jnp.zeros_like(x1)
        for c in range(D_FF // ff_chunk):
            a = jnp.dot(h, w1_ref[:, c * ff_chunk:(c + 1) * ff_chunk], preferred_element_type=F32)
            a = jnp.square(jnp.maximum(a, 0.0)).astype(BF16)
            f = f + jnp.dot(a, w2_ref[c * ff_chunk:(c + 1) * ff_chunk, :],
                            preferred_element_type=F32)
        o_ref[rows, :] = x1 + _rms(f, gpost2_ref[...])


def _const_spec(shape):
    nd = len(shape)
    return pl.BlockSpec(shape, lambda *_: (0,) * nd, pipeline_mode=pl.Buffered(1))


def _layer(x2, b, s, p):
    t = b * s
    rows = s // GRID_W
    tm_proj = 4 * SUB_ROWS
    tm_ffn = 2 * SUB_ROWS
    row = lambda a: a.reshape(1, -1).astype(F32)

    win = p["w_in"].astype(BF16)
    w_s = p["sg_w_s"].astype(BF16)
    wcat = jnp.concatenate([w_s[0::2], w_s[1::2]], axis=2)
    bs_full = jnp.repeat(p["sg_b_s"].astype(F32).T, SG_GROUP_DIM, axis=1)

    qkv, sgu = pl.pallas_call(
        _proj_kernel,
        grid=(t // tm_proj,),
        in_specs=[
            pl.BlockSpec((tm_proj, D_MODEL), lambda i: (i, 0)),
            _const_spec((1, D_MODEL)),
            _const_spec(win.shape),
            _const_spec((1, SG_WIDTH)),
            _const_spec((1, SG_WIDTH)),
            _const_spec(wcat.shape),
            _const_spec(bs_full.shape),
            _const_spec((1, SG_WIDTH)),
        ],
        out_specs=[
            pl.BlockSpec((3 * HEAD_PAIRS, tm_proj, LANES), lambda i: (0, i, 0)),
            pl.BlockSpec((tm_proj, SG_WIDTH), lambda i: (i, 0)),
        ],
        out_shape=[
            jax.ShapeDtypeStruct((3 * HEAD_PAIRS, t, LANES), BF16),
            jax.ShapeDtypeStruct((t, SG_WIDTH), BF16),
        ],
        compiler_params=pltpu.CompilerParams(
            dimension_semantics=("arbitrary",), vmem_limit_bytes=VMEM_LIMIT),
        name="proj_sgu",
    )(x2, row(p["norm_mix_pre"]), win, row(p["sg_ln_g"]), row(p["sg_ln_b"]), wcat, bs_full,
      row(p["g_out_sg"]))

    tab = _bias_table(p["na_rpb"], rows)
    plan = jnp.asarray(_bias_plan(rows)[1].reshape(-1))
    attn = pl.pallas_call(
        functools.partial(_attn_kernel, rows=rows),
        grid_spec=pltpu.PrefetchScalarGridSpec(
            num_scalar_prefetch=1,
            grid=(b,),
            in_specs=[
                pl.BlockSpec((3 * HEAD_PAIRS, s, LANES), lambda i, _: (0, i, 0)),
                _const_spec(tab.shape),
            ],
            out_specs=pl.BlockSpec((HEAD_PAIRS, s, LANES), lambda i, _: (0, i, 0)),
            scratch_shapes=[
                pltpu.VMEM((2 * QBLK, BAND), F32),
                pltpu.VMEM((2 * QBLK, BAND), F32),
                pltpu.VMEM((2 * QBLK, BAND), BF16),
                pltpu.VMEM((2 * QBLK, BAND), BF16),
                pltpu.VMEM((2 * QBLK, 1), F32),
                pltpu.VMEM((2 * QBLK, 1), F32),
            ],
        ),
        out_shape=jax.ShapeDtypeStruct((HEAD_PAIRS, t, LANES), BF16),
        compiler_params=pltpu.CompilerParams(
            dimension_semantics=("arbitrary",), vmem_limit_bytes=VMEM_LIMIT),
        name="natten",
    )(plan, qkv, tab)

    wout = p["w_out"].astype(BF16)
    w1 = p["w_ff1"].astype(BF16)
    w2 = p["w_ff2"].astype(BF16)
    out = pl.pallas_call(
        functools.partial(_ffn_kernel, ff_chunk=1024),
        grid=(t // tm_ffn,),
        in_specs=[
            pl.BlockSpec((tm_ffn, D_MODEL), lambda i: (i, 0)),
            pl.BlockSpec((HEAD_PAIRS, tm_ffn, LANES), lambda i: (0, i, 0)),
            pl.BlockSpec((tm_ffn, SG_WIDTH), lambda i: (i, 0)),
            _const_spec((1, NA_WIDTH)),
            _const_spec(wout.shape),
            _const_spec((1, D_MODEL)),
            _const_spec((1, D_MODEL)),
            _const_spec(w1.shape),
            _const_spec(w2.shape),
            _const_spec((1, D_MODEL)),
        ],
        out_specs=pl.BlockSpec((tm_ffn, D_MODEL), lambda i: (i, 0)),
        out_shape=jax.ShapeDtypeStruct((t, D_MODEL), F32),
        compiler_params=pltpu.CompilerParams(
            dimension_semantics=("arbitrary",), vmem_limit_bytes=VMEM_LIMIT),
        name="outproj_ffn",
    )(x2, attn, sgu, row(p["g_out_na"]), wout, row(p["norm_mix_post"]), row(p["norm_ffn_pre"]),
      w1, w2, row(p["norm_ffn_post"]))
    return out


def kernel(x, norm_mix_pre, w_in, na_rpb, sg_ln_g, sg_ln_b, sg_w_s, sg_b_s, g_out_na, g_out_sg,
           w_out, norm_mix_post, norm_ffn_pre, w_ff1, w_ff2, norm_ffn_post):
    b, s, d = x.shape
    assert d == D_MODEL and s % (QBLK * 2) == 0 and s // GRID_W >= 2 * NA_KH
    assert (b * s) % (4 * SUB_ROWS) == 0
    params = dict(norm_mix_pre=norm_mix_pre, w_in=w_in, na_rpb=na_rpb, sg_ln_g=sg_ln_g,
                  sg_ln_b=sg_ln_b, sg_w_s=sg_w_s, sg_b_s=sg_b_s, g_out_na=g_out_na,
                  g_out_sg=g_out_sg, w_out=w_out, norm_mix_post=norm_mix_post,
                  norm_ffn_pre=norm_ffn_pre, w_ff1=w_ff1, w_ff2=w_ff2, norm_ffn_post=norm_ffn_post)
    x2 = x.reshape(b * s, d)
    for l in range(norm_mix_pre.shape[0]):
        x2 = _layer(x2, b, s, {k: v[l] for k, v in params.items()})
    return x2.reshape(b, s, d)
```

```python
import functools

import numpy as np
import jax
import jax.numpy as jnp
from jax import lax
from jax.experimental import pallas as pl
from jax.experimental.pallas import tpu as pltpu

D_MODEL = 1024
GRID_W = 64
NA_HEADS = 8
NA_HEAD_DIM = 64
NA_WIDTH = NA_HEADS * NA_HEAD_DIM
NA_KH = 8
NA_KW = 16
SG_GROUPS = 8
SG_GROUP_DIM = 64
SG_WIDTH = SG_GROUPS * SG_GROUP_DIM
SG_CHUNK = 128
D_FF = 4 * D_MODEL
EPS = 1e-6

LANES = 128
HEAD_PAIRS = NA_HEADS // 2
QBLK_ROWS = 4
QBLK = QBLK_ROWS * GRID_W
BAND_TILES = 3
BAND = BAND_TILES * QBLK
BAND_PAIRS = BAND // LANES
SM_ROWS = 32
COL_HALVES = 2
CB_W = GRID_W // COL_HALVES
SUBLANES = 8
_KEY_RUNS = tuple((kr * GRID_W + cb * CB_W, cb * LANES + kr * CB_W)
                  for kr in range(QBLK_ROWS) for cb in range(COL_HALVES))
SUB_ROWS = 512
VMEM_LIMIT = 56 * 1024 * 1024

F32 = jnp.float32
BF16 = jnp.bfloat16


def _rms(x, g):
    return x * lax.rsqrt(jnp.mean(x * x, axis=-1, keepdims=True) + EPS) * g


def _proj_kernel(x_ref, gpre_ref, win_ref, lng_ref, lnb_ref, wcat_ref, bs_ref, gsg_ref,
                 qkv_ref, sgu_ref):
    for r0 in range(0, x_ref.shape[0], SUB_ROWS):
        _proj_rows(r0, x_ref, gpre_ref, win_ref, lng_ref, lnb_ref, wcat_ref, bs_ref, gsg_ref,
                   qkv_ref, sgu_ref)


def _proj_rows(r0, x_ref, gpre_ref, win_ref, lng_ref, lnb_ref, wcat_ref, bs_ref, gsg_ref,
               qkv_ref, sgu_ref):
    rows = slice(r0, r0 + SUB_ROWS)
    h = _rms(x_ref[rows, :], gpre_ref[...]).astype(BF16)

    def proj(c0, width):
        return jnp.dot(h, win_ref[:, c0:c0 + width], preferred_element_type=F32)

    scale = NA_HEAD_DIM ** -0.5
    for part in range(3):
        p = proj(part * NA_WIDTH, NA_WIDTH)
        if part == 0:
            p = p * scale
        for hp in range(HEAD_PAIRS):
            cols = p[:, hp * LANES:(hp + 1) * LANES].astype(BF16)
            if part == 0:
                qkv_ref[hp, rows, :] = cols
                continue
            for src, dst in _KEY_RUNS:
                for t0 in range(0, SUB_ROWS, QBLK):
                    qkv_ref[part * HEAD_PAIRS + hp, pl.ds(r0 + t0 + dst, CB_W), :] = (
                        cols[t0 + src:t0 + src + CB_W, :])

    su = jax.nn.gelu(proj(3 * NA_WIDTH, SG_WIDTH))
    sv = jax.nn.gelu(proj(3 * NA_WIDTH + SG_WIDTH, SG_WIDTH))
    mu = jnp.mean(sv, axis=-1, keepdims=True)
    xc = sv - mu
    v = xc * lax.rsqrt(jnp.mean(xc * xc, axis=-1, keepdims=True) + EPS)
    v = (v * lng_ref[...] + lnb_ref[...]).astype(BF16)

    lane = lax.broadcasted_iota(jnp.int32, (SG_CHUNK, LANES), 1)
    first = lane < SG_GROUP_DIM
    zero = jnp.zeros((SG_CHUNK, LANES), BF16)
    chunks = []
    for c in range(SUB_ROWS // SG_CHUNK):
        cols = []
        for gp in range(SG_GROUPS // 2):
            v128 = v[c * SG_CHUNK:(c + 1) * SG_CHUNK, gp * LANES:(gp + 1) * LANES]
            rhs = jnp.concatenate([jnp.where(first, v128, zero), jnp.where(first, zero, v128)], axis=0)
            cols.append(jnp.dot(wcat_ref[gp], rhs, preferred_element_type=F32))
        chunks.append(jnp.concatenate(cols, axis=1) + bs_ref[...])
    mixed = jnp.concatenate(chunks, axis=0)
    sgu_ref[rows, :] = _rms(su * mixed, gsg_ref[...]).astype(BF16)


def _row_valid(r, kr, rows):
    rs = min(max(r - NA_KH // 2, 0), rows - NA_KH)
    return 0 <= kr < rows and rs <= kr < rs + NA_KH


def _col_ok():
    qc = np.arange(GRID_W)[:, None]
    kc = np.arange(GRID_W)[None, :]
    col_start = np.clip(qc - NA_KW // 2, 0, GRID_W - NA_KW)
    return (kc >= col_start) & (kc < col_start + NA_KW)


def _minor_offset(part):
    ok = _col_ok()
    own = slice(part * CB_W, (part + 1) * CB_W)
    other = slice((1 - part) * CB_W, (2 - part) * CB_W)
    groups = [g for g in range(CB_W // SUBLANES)
              if ok[part * CB_W + g * SUBLANES:part * CB_W + (g + 1) * SUBLANES, other].any()]
    assert len(groups) == 1 and ok[own, own].any(axis=1).all()
    return groups[0] * SUBLANES


def _bias_plan(rows):
    nblk = rows // QBLK_ROWS
    pieces = {}
    plan = []
    for j in range(nblk):
        kt0 = min(max(j - 1, 0), nblk - BAND_TILES)
        table = []
        for qr in range(QBLK_ROWS):
            r = j * QBLK_ROWS + qr
            for t in range(BAND_TILES):
                kr0 = (kt0 + t) * QBLK_ROWS
                valid = tuple(_row_valid(r, kr0 + kr, rows) for kr in range(QBLK_ROWS))
                for cb in range(COL_HALVES):
                    table.append(pieces.setdefault((kr0 - r, valid, cb), len(pieces)))
        plan.append(table)
    return list(pieces.keys()), np.asarray(plan, np.int32)


def _bias_table(rpb, rows):
    keys, _ = _bias_plan(rows)
    rpb = rpb.astype(F32)
    h, ndr, ndc = rpb.shape
    pad = GRID_W - NA_KW
    ext = jnp.concatenate([jnp.broadcast_to(rpb[..., :1], (h, ndr, pad)), rpb,
                           jnp.broadcast_to(rpb[..., -1:], (h, ndr, pad))], axis=-1)
    toep = jnp.stack([ext[..., GRID_W - 1 - qc:2 * GRID_W - 1 - qc] for qc in range(GRID_W)],
                     axis=2)
    masked = jnp.where(_col_ok(), toep, -jnp.inf)
    dead = jnp.full((h, GRID_W, CB_W), -jnp.inf, F32)
    pieces = []
    for d, valid, cb in keys:
        runs = [masked[:, d + kr + NA_KH - 1, :, cb * CB_W:(cb + 1) * CB_W] if ok else dead
                for kr, ok in enumerate(valid)]
        pieces.append(jnp.concatenate(runs, axis=-1))
    return jnp.stack(pieces, axis=1)


def _attn_kernel(plan_ref, qkv_ref, tab_ref, o_ref, s0_ref, s1_ref, p0_ref, p1_ref, l0_ref, l1_ref,
                 *, rows):
    nblk = rows // QBLK_ROWS
    n_items = HEAD_PAIRS * nblk
    lane = lax.broadcasted_iota(jnp.int32, (QBLK, LANES), 1)
    first = lane < NA_HEAD_DIM

    def coords(i):
        hp = i // nblk
        j = i - hp * nblk
        q0 = pl.multiple_of(j * QBLK, QBLK)
        k0 = pl.multiple_of(jnp.clip(j - 1, 0, nblk - BAND_TILES) * QBLK, QBLK)
        return hp, j, q0, k0

    def scores(i, s_ref):
        hp, _, q0, k0 = coords(i)
        q = qkv_ref[hp, pl.ds(q0, QBLK), :]
        k = qkv_ref[HEAD_PAIRS + hp, pl.ds(k0, BAND), :]
        zero = jnp.zeros_like(q)
        qq = jnp.concatenate([jnp.where(first, q, zero), jnp.where(first, zero, q)], axis=0)
        s_ref[...] = lax.dot_general(qq, k, (((1,), (1,)), ((), ())), preferred_element_type=F32)

    def softmax(i, s_ref, p_ref, l_ref):
        hp, j, _, _ = coords(i)
        n_chunks = 2 * QBLK // SM_ROWS
        halves_per_row = GRID_W // SM_ROWS

        def chunk(r):
            hh, rem = divmod(r, QBLK // SM_ROWS)
            qr, part = divmod(rem, halves_per_row)
            mo = _minor_offset(part)
            r0 = r * SM_ROWS

            def gather(cb, lo, n):
                lts = [COL_HALVES * t + cb for t in range(BAND_TILES)]
                sv = jnp.concatenate(
                    [s_ref[r0 + lo:r0 + lo + n, lt * LANES:(lt + 1) * LANES] for lt in lts], axis=1)
                bias = jnp.concatenate(
                    [tab_ref[2 * hp + hh, plan_ref[(j * QBLK_ROWS + qr) * BAND_PAIRS + lt],
                             part * SM_ROWS + lo:part * SM_ROWS + lo + n, :] for lt in lts], axis=1)
                return sv, bias, lts

            return (r0, mo, functools.partial(gather, part, 0, SM_ROWS),
                    functools.partial(gather, 1 - part, mo, SUBLANES))

        def fold(x, op):
            out = x[:, :LANES]
            for t in range(1, BAND_TILES):
                out = op(out, x[:, t * LANES:(t + 1) * LANES])
            return out

        def merge(main, minor, mo, op):
            parts = [main[:mo], op(main[mo:mo + SUBLANES], minor), main[mo + SUBLANES:]]
            return jnp.concatenate([x for x in parts if x.shape[0]], axis=0)

        maxes = []
        for r in range(n_chunks):
            _, mo, main, minor = chunk(r)
            sv, bias, _ = main()
            m_main = fold(sv + bias, jnp.maximum)
            sv, bias, _ = minor()
            m_minor = fold(sv + bias, jnp.maximum)
            maxes.append(jnp.max(merge(m_main, m_minor, mo, jnp.maximum), axis=-1, keepdims=True))
        for r in range(n_chunks):
            r0, mo, main, minor = chunk(r)
            m = maxes[r]
            sv, bias, lts = main()
            e = jnp.exp(sv + (bias - m))
            l_main = fold(e, jnp.add)
            for n, lt in enumerate(lts):
                p_ref[r0:r0 + SM_ROWS, lt * LANES:(lt + 1) * LANES] = (
                    e[:, n * LANES:(n + 1) * LANES].astype(BF16))
            sv, bias, lts = minor()
            e = jnp.exp(sv + (bias - m[mo:mo + SUBLANES]))
            l_ref[r0:r0 + SM_ROWS, :] = jnp.sum(merge(l_main, fold(e, jnp.add), mo, jnp.add),
                                                axis=-1, keepdims=True)
            pad = jnp.zeros_like(e)
            first_of_pair = mo % (2 * SUBLANES) == 0
            blk = jnp.concatenate([e, pad] if first_of_pair else [pad, e], axis=0).astype(BF16)
            b0 = r0 + mo - (0 if first_of_pair else SUBLANES)
            for n, lt in enumerate(lts):
                p_ref[b0:b0 + 2 * SUBLANES, lt * LANES:(lt + 1) * LANES] = blk[:, n * LANES:(n + 1) * LANES]

    def pv(i, p_ref, l_ref):
        hp, _, q0, k0 = coords(i)
        v = qkv_ref[2 * HEAD_PAIRS + hp, pl.ds(k0, BAND), :]
        o = jnp.dot(p_ref[...], v, preferred_element_type=F32) / l_ref[...]
        o_ref[hp, pl.ds(q0, QBLK), :] = jnp.where(first, o[:QBLK], o[QBLK:]).astype(BF16)

    p0_ref[...] = jnp.zeros_like(p0_ref)
    p1_ref[...] = jnp.zeros_like(p1_ref)
    scores(0, s0_ref)
    scores(1, s1_ref)
    softmax(0, s0_ref, p0_ref, l0_ref)

    def body(t, carry):
        i = 2 * t
        pv(i, p0_ref, l0_ref)
        scores(i + 2, s0_ref)
        softmax(i + 1, s1_ref, p1_ref, l1_ref)
        pv(i + 1, p1_ref, l1_ref)
        scores(i + 3, s1_ref)
        softmax(i + 2, s0_ref, p0_ref, l0_ref)
        return carry

    lax.fori_loop(0, n_items // 2 - 1, body, 0)
    pv(n_items - 2, p0_ref, l0_ref)
    softmax(n_items - 1, s1_ref, p1_ref, l1_ref)
    pv(n_items - 1, p1_ref, l1_ref)


def _ffn_kernel(x_ref, a_ref, s_ref, gna_ref, wout_ref, gpost_ref, gpre2_ref, w1_ref, w2_ref,
                gpost2_ref, o_ref, *, ff_chunk):
    for r0 in range(0, x_ref.shape[0], SUB_ROWS):
        rows = slice(r0, r0 + SUB_ROWS)
        attn = jnp.concatenate([a_ref[hp, rows, :] for hp in range(HEAD_PAIRS)], axis=1).astype(F32)
        y = jnp.dot(_rms(attn, gna_ref[...]).astype(BF16), wout_ref[:NA_WIDTH, :],
                    preferred_element_type=F32)
        y = y + jnp.dot(s_ref[rows, :], wout_ref[NA_WIDTH:, :], preferred_element_type=F32)
        x1 = x_ref[rows, :] + _rms(y, gpost_ref[...])
        h = _rms(x1, gpre2_ref[...]).astype(BF16)
        f = jnp.zeros_like(x1)
        for c in range(D_FF // ff_chunk):
            a = jnp.dot(h, w1_ref[:, c * ff_chunk:(c + 1) * ff_chunk], preferred_element_type=F32)
            a = jnp.square(jnp.maximum(a, 0.0)).astype(BF16)
            f = f + jnp.dot(a, w2_ref[c * ff_chunk:(c + 1) * ff_chunk, :],
                            preferred_element_type=F32)
        o_ref[rows, :] = x1 + _rms(f, gpost2_ref[...])


def _const_spec(shape):
    nd = len(shape)
    return pl.BlockSpec(shape, lambda *_: (0,) * nd, pipeline_mode=pl.Buffered(1))


def _layer(x2, b, s, p):
    t = b * s
    rows = s // GRID_W
    tm_proj = 4 * SUB_ROWS
    tm_ffn = 2 * SUB_ROWS
    row = lambda a: a.reshape(1, -1).astype(F32)

    win = p["w_in"].astype(BF16)
    w_s = p["sg_w_s"].astype(BF16)
    wcat = jnp.concatenate([w_s[0::2], w_s[1::2]], axis=2)
    bs_full = jnp.repeat(p["sg_b_s"].astype(F32).T, SG_GROUP_DIM, axis=1)

    qkv, sgu = pl.pallas_call(
        _proj_kernel,
        grid=(t // tm_proj,),
        in_specs=[
            pl.BlockSpec((tm_proj, D_MODEL), lambda i: (i, 0)),
            _const_spec((1, D_MODEL)),
            _const_spec(win.shape),
            _const_spec((1, SG_WIDTH)),
            _const_spec((1, SG_WIDTH)),
            _const_spec(wcat.shape),
            _const_spec(bs_full.shape),
            _const_spec((1, SG_WIDTH)),
        ],
        out_specs=[
            pl.BlockSpec((3 * HEAD_PAIRS, tm_proj, LANES), lambda i: (0, i, 0)),
            pl.BlockSpec((tm_proj, SG_WIDTH), lambda i: (i, 0)),
        ],
        out_shape=[
            jax.ShapeDtypeStruct((3 * HEAD_PAIRS, t, LANES), BF16),
            jax.ShapeDtypeStruct((t, SG_WIDTH), BF16),
        ],
        compiler_params=pltpu.CompilerParams(
            dimension_semantics=("arbitrary",), vmem_limit_bytes=VMEM_LIMIT),
        name="proj_sgu",
    )(x2, row(p["norm_mix_pre"]), win, row(p["sg_ln_g"]), row(p["sg_ln_b"]), wcat, bs_full,
      row(p["g_out_sg"]))

    tab = _bias_table(p["na_rpb"], rows)
    plan = jnp.asarray(_bias_plan(rows)[1].reshape(-1))
    attn = pl.pallas_call(
        functools.partial(_attn_kernel, rows=rows),
        grid_spec=pltpu.PrefetchScalarGridSpec(
            num_scalar_prefetch=1,
            grid=(b,),
            in_specs=[
                pl.BlockSpec((3 * HEAD_PAIRS, s, LANES), lambda i, _: (0, i, 0)),
                _const_spec(tab.shape),
            ],
            out_specs=pl.BlockSpec((HEAD_PAIRS, s, LANES), lambda i, _: (0, i, 0)),
            scratch_shapes=[
                pltpu.VMEM((2 * QBLK, BAND), F32),
                pltpu.VMEM((2 * QBLK, BAND), F32),
                pltpu.VMEM((2 * QBLK, BAND), BF16),
                pltpu.VMEM((2 * QBLK, BAND), BF16),
                pltpu.VMEM((2 * QBLK, 1), F32),
                pltpu.VMEM((2 * QBLK, 1), F32),
            ],
        ),
        out_shape=jax.ShapeDtypeStruct((HEAD_PAIRS, t, LANES), BF16),
        compiler_params=pltpu.CompilerParams(
            dimension_semantics=("arbitrary",), vmem_limit_bytes=VMEM_LIMIT),
        name="natten",
    )(plan, qkv, tab)

    wout = p["w_out"].astype(BF16)
    w1 = p["w_ff1"].astype(BF16)
    w2 = p["w_ff2"].astype(BF16)
    out = pl.pallas_call(
        functools.partial(_ffn_kernel, ff_chunk=1024),
        grid=(t // tm_ffn,),
        in_specs=[
            pl.BlockSpec((tm_ffn, D_MODEL), lambda i: (i, 0)),
            pl.BlockSpec((HEAD_PAIRS, tm_ffn, LANES), lambda i: (0, i, 0)),
            pl.BlockSpec((tm_ffn, SG_WIDTH), lambda i: (i, 0)),
            _const_spec((1, NA_WIDTH)),
            _const_spec(wout.shape),
            _const_spec((1, D_MODEL)),
            _const_spec((1, D_MODEL)),
            _const_spec(w1.shape),
            _const_spec(w2.shape),
            _const_spec((1, D_MODEL)),
        ],
        out_specs=pl.BlockSpec((tm_ffn, D_MODEL), lambda i: (i, 0)),
        out_shape=jax.ShapeDtypeStruct((t, D_MODEL), F32),
        compiler_params=pltpu.CompilerParams(
            dimension_semantics=("arbitrary",), vmem_limit_bytes=VMEM_LIMIT),
        name="outproj_ffn",
    )(x2, attn, sgu, row(p["g_out_na"]), wout, row(p["norm_mix_post"]), row(p["norm_ffn_pre"]),
      w1, w2, row(p["norm_ffn_post"]))
    return out


def kernel(x, norm_mix_pre, w_in, na_rpb, sg_ln_g, sg_ln_b, sg_w_s, sg_b_s, g_out_na, g_out_sg,
           w_out, norm_mix_post, norm_ffn_pre, w_ff1, w_ff2, norm_ffn_post):
    b, s, d = x.shape
    assert d == D_MODEL and s % (QBLK * 2) == 0 and s // GRID_W >= 2 * NA_KH
    assert (b * s) % (4 * SUB_ROWS) == 0
    params = dict(norm_mix_pre=norm_mix_pre, w_in=w_in, na_rpb=na_rpb, sg_ln_g=sg_ln_g,
                  sg_ln_b=sg_ln_b, sg_w_s=sg_w_s, sg_b_s=sg_b_s, g_out_na=g_out_na,
                  g_out_sg=g_out_sg, w_out=w_out, norm_mix_post=norm_mix_post,
                  norm_ffn_pre=norm_ffn_pre, w_ff1=w_ff1, w_ff2=w_ff2, norm_ffn_post=norm_ffn_post)
    x2 = x.reshape(b * s, d)
    for l in range(norm_mix_pre.shape[0]):
        x2 = _layer(x2, b, s, {k: v[l] for k, v in params.items()})
    return x2.reshape(b, s, d)
```

```python
import functools

import numpy as np
import jax
import jax.numpy as jnp
from jax import lax
from jax.experimental import pallas as pl
from jax.experimental.pallas import tpu as pltpu

D_MODEL = 1024
GRID_W = 64
NA_HEADS = 8
NA_HEAD_DIM = 64
NA_WIDTH = NA_HEADS * NA_HEAD_DIM
NA_KH = 8
NA_KW = 16
SG_GROUPS = 8
SG_GROUP_DIM = 64
SG_WIDTH = SG_GROUPS * SG_GROUP_DIM
SG_CHUNK = 128
D_FF = 4 * D_MODEL
EPS = 1e-6

LANES = 128
HEAD_PAIRS = NA_HEADS // 2
QBLK_ROWS = 4
QBLK = QBLK_ROWS * GRID_W
BAND_TILES = 3
BAND = BAND_TILES * QBLK
BAND_PAIRS = BAND // LANES
SM_ROWS = 32
COL_HALVES = 2
CB_W = GRID_W // COL_HALVES
SUBLANES = 8
_KEY_RUNS = tuple((kr * GRID_W + cb * CB_W, cb * LANES + kr * CB_W)
                  for kr in range(QBLK_ROWS) for cb in range(COL_HALVES))
SUB_ROWS = 512
VMEM_LIMIT = 56 * 1024 * 1024

F32 = jnp.float32
BF16 = jnp.bfloat16


def _rms(x, g):
    return x * lax.rsqrt(jnp.mean(x * x, axis=-1, keepdims=True) + EPS) * g


def _proj_kernel(x_ref, gpre_ref, win_ref, lng_ref, lnb_ref, wcat_ref, bs_ref, gsg_ref,
                 qkv_ref, sgu_ref):
    for r0 in range(0, x_ref.shape[0], SUB_ROWS):
        _proj_rows(r0, x_ref, gpre_ref, win_ref, lng_ref, lnb_ref, wcat_ref, bs_ref, gsg_ref,
                   qkv_ref, sgu_ref)


def _proj_rows(r0, x_ref, gpre_ref, win_ref, lng_ref, lnb_ref, wcat_ref, bs_ref, gsg_ref,
               qkv_ref, sgu_ref):
    rows = slice(r0, r0 + SUB_ROWS)
    h = _rms(x_ref[rows, :], gpre_ref[...]).astype(BF16)

    def proj(c0, width):
        return jnp.dot(h, win_ref[:, c0:c0 + width], preferred_element_type=F32)

    scale = NA_HEAD_DIM ** -0.5
    for part in range(3):
        p = proj(part * NA_WIDTH, NA_WIDTH)
        if part == 0:
            p = p * scale
        for hp in range(HEAD_PAIRS):
            cols = p[:, hp * LANES:(hp + 1) * LANES].astype(BF16)
            if part == 0:
                qkv_ref[hp, rows, :] = cols
                continue
            for src, dst in _KEY_RUNS:
                for t0 in range(0, SUB_ROWS, QBLK):
                    qkv_ref[part * HEAD_PAIRS + hp, pl.ds(r0 + t0 + dst, CB_W), :] = (
                        cols[t0 + src:t0 + src + CB_W, :])

    su = jax.nn.gelu(proj(3 * NA_WIDTH, SG_WIDTH))
    sv = jax.nn.gelu(proj(3 * NA_WIDTH + SG_WIDTH, SG_WIDTH))
    mu = jnp.mean(sv, axis=-1, keepdims=True)
    xc = sv - mu
    v = xc * lax.rsqrt(jnp.mean(xc * xc, axis=-1, keepdims=True) + EPS)
    v = (v * lng_ref[...] + lnb_ref[...]).astype(BF16)

    lane = lax.broadcasted_iota(jnp.int32, (SG_CHUNK, LANES), 1)
    first = lane < SG_GROUP_DIM
    zero = jnp.zeros((SG_CHUNK, LANES), BF16)
    n_chunks = SUB_ROWS // SG_CHUNK
    outs = []
    for gp in range(SG_GROUPS // 2):
        rhs = []
        for c in range(n_chunks):
            v128 = v[c * SG_CHUNK:(c + 1) * SG_CHUNK, gp * LANES:(gp + 1) * LANES]
            rhs.append(jnp.concatenate([jnp.where(first, v128, zero), jnp.where(first, zero, v128)],
                                       axis=0))
        outs.append(jnp.dot(wcat_ref[gp], jnp.concatenate(rhs, axis=1), preferred_element_type=F32))
    mixed = jnp.concatenate(
        [jnp.concatenate([o[:, c * LANES:(c + 1) * LANES] for o in outs], axis=1) + bs_ref[...]
         for c in range(n_chunks)], axis=0)
    sgu_ref[rows, :] = _rms(su * mixed, gsg_ref[...]).astype(BF16)


def _row_valid(r, kr, rows):
    rs = min(max(r - NA_KH // 2, 0), rows - NA_KH)
    return 0 <= kr < rows and rs <= kr < rs + NA_KH


def _col_ok():
    qc = np.arange(GRID_W)[:, None]
    kc = np.arange(GRID_W)[None, :]
    col_start = np.clip(qc - NA_KW // 2, 0, GRID_W - NA_KW)
    return (kc >= col_start) & (kc < col_start + NA_KW)


def _minor_offset(part):
    ok = _col_ok()
    own = slice(part * CB_W, (part + 1) * CB_W)
    other = slice((1 - part) * CB_W, (2 - part) * CB_W)
    groups = [g for g in range(CB_W // SUBLANES)
              if ok[part * CB_W + g * SUBLANES:part * CB_W + (g + 1) * SUBLANES, other].any()]
    assert len(groups) == 1 and ok[own, own].any(axis=1).all()
    return groups[0] * SUBLANES


def _bias_plan(rows):
    nblk = rows // QBLK_ROWS
    pieces = {}
    plan = []
    for j in range(nblk):
        kt0 = min(max(j - 1, 0), nblk - BAND_TILES)
        table = []
        for qr in range(QBLK_ROWS):
            r = j * QBLK_ROWS + qr
            for t in range(BAND_TILES):
                kr0 = (kt0 + t) * QBLK_ROWS
                valid = tuple(_row_valid(r, kr0 + kr, rows) for kr in range(QBLK_ROWS))
                for cb in range(COL_HALVES):
                    table.append(pieces.setdefault((kr0 - r, valid, cb), len(pieces)))
        plan.append(table)
    return list(pieces.keys()), np.asarray(plan, np.int32)


def _bias_table(rpb, rows):
    keys, _ = _bias_plan(rows)
    rpb = rpb.astype(F32)
    h, ndr, ndc = rpb.shape
    pad = GRID_W - NA_KW
    ext = jnp.concatenate([jnp.broadcast_to(rpb[..., :1], (h, ndr, pad)), rpb,
                           jnp.broadcast_to(rpb[..., -1:], (h, ndr, pad))], axis=-1)
    toep = jnp.stack([ext[..., GRID_W - 1 - qc:2 * GRID_W - 1 - qc] for qc in range(GRID_W)],
                     axis=2)
    masked = jnp.where(_col_ok(), toep, -jnp.inf)
    dead = jnp.full((h, GRID_W, CB_W), -jnp.inf, F32)
    pieces = []
    for d, valid, cb in keys:
        runs = [masked[:, d + kr + NA_KH - 1, :, cb * CB_W:(cb + 1) * CB_W] if ok else dead
                for kr, ok in enumerate(valid)]
        pieces.append(jnp.concatenate(runs, axis=-1))
    return jnp.stack(pieces, axis=1)


def _attn_kernel(plan_ref, qkv_ref, tab_ref, o_ref, s0_ref, s1_ref, p0_ref, p1_ref, l0_ref, l1_ref,
                 *, rows):
    nblk = rows // QBLK_ROWS
    n_items = HEAD_PAIRS * nblk
    lane = lax.broadcasted_iota(jnp.int32, (QBLK, LANES), 1)
    first = lane < NA_HEAD_DIM

    def coords(i):
        hp = i // nblk
        j = i - hp * nblk
        q0 = pl.multiple_of(j * QBLK, QBLK)
        k0 = pl.multiple_of(jnp.clip(j - 1, 0, nblk - BAND_TILES) * QBLK, QBLK)
        return hp, j, q0, k0

    def scores(i, s_ref):
        hp, _, q0, k0 = coords(i)
        q = qkv_ref[hp, pl.ds(q0, QBLK), :]
        k = qkv_ref[HEAD_PAIRS + hp, pl.ds(k0, BAND), :]
        zero = jnp.zeros_like(q)
        qq = jnp.concatenate([jnp.where(first, q, zero), jnp.where(first, zero, q)], axis=0)
        s_ref[...] = lax.dot_general(qq, k, (((1,), (1,)), ((), ())), preferred_element_type=F32)

    def softmax(i, s_ref, p_ref, l_ref):
        hp, j, _, _ = coords(i)
        n_chunks = 2 * QBLK // SM_ROWS
        halves_per_row = GRID_W // SM_ROWS

        def chunk(r):
            hh, rem = divmod(r, QBLK // SM_ROWS)
            qr, part = divmod(rem, halves_per_row)
            mo = _minor_offset(part)
            r0 = r * SM_ROWS

            def gather(cb, lo, n):
                lts = [COL_HALVES * t + cb for t in range(BAND_TILES)]
                sv = jnp.concatenate(
                    [s_ref[r0 + lo:r0 + lo + n, lt * LANES:(lt + 1) * LANES] for lt in lts], axis=1)
                bias = jnp.concatenate(
                    [tab_ref[2 * hp + hh, plan_ref[(j * QBLK_ROWS + qr) * BAND_PAIRS + lt],
                             part * SM_ROWS + lo:part * SM_ROWS + lo + n, :] for lt in lts], axis=1)
                return sv, bias, lts

            return (r0, mo, functools.partial(gather, part, 0, SM_ROWS),
                    functools.partial(gather, 1 - part, mo, SUBLANES))

        def fold(x, op):
            out = x[:, :LANES]
            for t in range(1, BAND_TILES):
                out = op(out, x[:, t * LANES:(t + 1) * LANES])
            return out

        def merge(main, minor, mo, op):
            parts = [main[:mo], op(main[mo:mo + SUBLANES], minor), main[mo + SUBLANES:]]
            return jnp.concatenate([x for x in parts if x.shape[0]], axis=0)

        maxes = []
        for r in range(n_chunks):
            _, mo, main, minor = chunk(r)
            sv, bias, _ = main()
            m_main = fold(sv + bias, jnp.maximum)
            sv, bias, _ = minor()
            m_minor = fold(sv + bias, jnp.maximum)
            maxes.append(jnp.max(merge(m_main, m_minor, mo, jnp.maximum), axis=-1, keepdims=True))
        for r in range(n_chunks):
            r0, mo, main, minor = chunk(r)
            m = maxes[r]
            sv, bias, lts = main()
            e = jnp.exp(sv + (bias - m))
            l_main = fold(e, jnp.add)
            for n, lt in enumerate(lts):
                p_ref[r0:r0 + SM_ROWS, lt * LANES:(lt + 1) * LANES] = (
                    e[:, n * LANES:(n + 1) * LANES].astype(BF16))
            sv, bias, lts = minor()
            e = jnp.exp(sv + (bias - m[mo:mo + SUBLANES]))
            l_ref[r0:r0 + SM_ROWS, :] = jnp.sum(merge(l_main, fold(e, jnp.add), mo, jnp.add),
                                                axis=-1, keepdims=True)
            pad = jnp.zeros_like(e)
            first_of_pair = mo % (2 * SUBLANES) == 0
            blk = jnp.concatenate([e, pad] if first_of_pair else [pad, e], axis=0).astype(BF16)
            b0 = r0 + mo - (0 if first_of_pair else SUBLANES)
            for n, lt in enumerate(lts):
                p_ref[b0:b0 + 2 * SUBLANES, lt * LANES:(lt + 1) * LANES] = blk[:, n * LANES:(n + 1) * LANES]

    def pv(i, p_ref, l_ref):
        hp, _, q0, k0 = coords(i)
        v = qkv_ref[2 * HEAD_PAIRS + hp, pl.ds(k0, BAND), :]
        o = jnp.dot(p_ref[...], v, preferred_element_type=F32) / l_ref[...]
        o_ref[hp, pl.ds(q0, QBLK), :] = jnp.where(first, o[:QBLK], o[QBLK:]).astype(BF16)

    p0_ref[...] = jnp.zeros_like(p0_ref)
    p1_ref[...] = jnp.zeros_like(p1_ref)
    scores(0, s0_ref)
    scores(1, s1_ref)
    softmax(0, s0_ref, p0_ref, l0_ref)

    def body(t, carry):
        i = 2 * t
        pv(i, p0_ref, l0_ref)
        scores(i + 2, s0_ref)
        softmax(i + 1, s1_ref, p1_ref, l1_ref)
        pv(i + 1, p1_ref, l1_ref)
        scores(i + 3, s1_ref)
        softmax(i + 2, s0_ref, p0_ref, l0_ref)
        return carry

    lax.fori_loop(0, n_items // 2 - 1, body, 0)
    pv(n_items - 2, p0_ref, l0_ref)
    softmax(n_items - 1, s1_ref, p1_ref, l1_ref)
    pv(n_items - 1, p1_ref, l1_ref)


def _ffn_kernel(x_ref, a_ref, s_ref, gna_ref, wout_ref, gpost_ref, gpre2_ref, w1_ref, w2_ref,
                gpost2_ref, o_ref, *, ff_chunk):
    for r0 in range(0, x_ref.shape[0], SUB_ROWS):
        rows = slice(r0, r0 + SUB_ROWS)
        attn = jnp.concatenate([a_ref[hp, rows, :] for hp in range(HEAD_PAIRS)], axis=1).astype(F32)
        y = jnp.dot(_rms(attn, gna_ref[...]).astype(BF16), wout_ref[:NA_WIDTH, :],
                    preferred_element_type=F32)
        y = y + jnp.dot(s_ref[rows, :], wout_ref[NA_WIDTH:, :], preferred_element_type=F32)
        x1 = x_ref[rows, :] + _rms(y, gpost_ref[...])
        h = _rms(x1, gpre2_ref[...]).astype(BF16)
        f = jnp.zeros_like(x1)
        for c in range(D_FF // ff_chunk):
            a = jnp.dot(h, w1_ref[:, c * ff_chunk:(c + 1) * ff_chunk], preferred_element_type=F32)
            a = jnp.square(jnp.maximum(a, 0.0)).astype(BF16)
            f = f + jnp.dot(a, w2_ref[c * ff_chunk:(c + 1) * ff_chunk, :],
                            preferred_element_type=F32)
        o_ref[rows, :] = x1 + _rms(f, gpost2_ref[...])


def _const_spec(shape):
    nd = len(shape)
    return pl.BlockSpec(shape, lambda *_: (0,) * nd, pipeline_mode=pl.Buffered(1))


def _layer(x2, b, s, p):
    t = b * s
    rows = s // GRID_W
    tm_proj = 4 * SUB_ROWS
    tm_ffn = 2 * SUB_ROWS
    row = lambda a: a.reshape(1, -1).astype(F32)

    win = p["w_in"].astype(BF16)
    w_s = p["sg_w_s"].astype(BF16)
    wcat = jnp.concatenate([w_s[0::2], w_s[1::2]], axis=2)
    bs_full = jnp.repeat(p["sg_b_s"].astype(F32).T, SG_GROUP_DIM, axis=1)

    qkv, sgu = pl.pallas_call(
        _proj_kernel,
        grid=(t // tm_proj,),
        in_specs=[
            pl.BlockSpec((tm_proj, D_MODEL), lambda i: (i, 0)),
            _const_spec((1, D_MODEL)),
            _const_spec(win.shape),
            _const_spec((1, SG_WIDTH)),
            _const_spec((1, SG_WIDTH)),
            _const_spec(wcat.shape),
            _const_spec(bs_full.shape),
            _const_spec((1, SG_WIDTH)),
        ],
        out_specs=[
            pl.BlockSpec((3 * HEAD_PAIRS, tm_proj, LANES), lambda i: (0, i, 0)),
            pl.BlockSpec((tm_proj, SG_WIDTH), lambda i: (i, 0)),
        ],
        out_shape=[
            jax.ShapeDtypeStruct((3 * HEAD_PAIRS, t, LANES), BF16),
            jax.ShapeDtypeStruct((t, SG_WIDTH), BF16),
        ],
        compiler_params=pltpu.CompilerParams(
            dimension_semantics=("arbitrary",), vmem_limit_bytes=VMEM_LIMIT),
        name="proj_sgu",
    )(x2, row(p["norm_mix_pre"]), win, row(p["sg_ln_g"]), row(p["sg_ln_b"]), wcat, bs_full,
      row(p["g_out_sg"]))

    tab = _bias_table(p["na_rpb"], rows)
    plan = jnp.asarray(_bias_plan(rows)[1].reshape(-1))
    attn = pl.pallas_call(
        functools.partial(_attn_kernel, rows=rows),
        grid_spec=pltpu.PrefetchScalarGridSpec(
            num_scalar_prefetch=1,
            grid=(b,),
            in_specs=[
                pl.BlockSpec((3 * HEAD_PAIRS, s, LANES), lambda i, _: (0, i, 0)),
                _const_spec(tab.shape),
            ],
            out_specs=pl.BlockSpec((HEAD_PAIRS, s, LANES), lambda i, _: (0, i, 0)),
            scratch_shapes=[
                pltpu.VMEM((2 * QBLK, BAND), F32),
                pltpu.VMEM((2 * QBLK, BAND), F32),
                pltpu.VMEM((2 * QBLK, BAND), BF16),
                pltpu.VMEM((2 * QBLK, BAND), BF16),
                pltpu.VMEM((2 * QBLK, 1), F32),
                pltpu.VMEM((2 * QBLK, 1), F32),
            ],
        ),
        out_shape=jax.ShapeDtypeStruct((HEAD_PAIRS, t, LANES), BF16),
        compiler_params=pltpu.CompilerParams(
            dimension_semantics=("arbitrary",), vmem_limit_bytes=VMEM_LIMIT),
        name="natten",
    )(plan, qkv, tab)

    wout = p["w_out"].astype(BF16)
    w1 = p["w_ff1"].astype(BF16)
    w2 = p["w_ff2"].astype(BF16)
    out = pl.pallas_call(
        functools.partial(_ffn_kernel, ff_chunk=1024),
        grid=(t // tm_ffn,),
        in_specs=[
            pl.BlockSpec((tm_ffn, D_MODEL), lambda i: (i, 0)),
            pl.BlockSpec((HEAD_PAIRS, tm_ffn, LANES), lambda i: (0, i, 0)),
            pl.BlockSpec((tm_ffn, SG_WIDTH), lambda i: (i, 0)),
            _const_spec((1, NA_WIDTH)),
            _const_spec(wout.shape),
            _const_spec((1, D_MODEL)),
            _const_spec((1, D_MODEL)),
            _const_spec(w1.shape),
            _const_spec(w2.shape),
            _const_spec((1, D_MODEL)),
        ],
        out_specs=pl.BlockSpec((tm_ffn, D_MODEL), lambda i: (i, 0)),
        out_shape=jax.ShapeDtypeStruct((t, D_MODEL), F32),
        compiler_params=pltpu.CompilerParams(
            dimension_semantics=("arbitrary",), vmem_limit_bytes=VMEM_LIMIT),
        name="outproj_ffn",
    )(x2, attn, sgu, row(p["g_out_na"]), wout, row(p["norm_mix_post"]), row(p["norm_ffn_pre"]),
      w1, w2, row(p["norm_ffn_post"]))
    return out


def kernel(x, norm_mix_pre, w_in, na_rpb, sg_ln_g, sg_ln_b, sg_w_s, sg_b_s, g_out_na, g_out_sg,
           w_out, norm_mix_post, norm_ffn_pre, w_ff1, w_ff2, norm_ffn_post):
    b, s, d = x.shape
    assert d == D_MODEL and s % (QBLK * 2) == 0 and s // GRID_W >= 2 * NA_KH
    assert (b * s) % (4 * SUB_ROWS) == 0
    params = dict(norm_mix_pre=norm_mix_pre, w_in=w_in, na_rpb=na_rpb, sg_ln_g=sg_ln_g,
                  sg_ln_b=sg_ln_b, sg_w_s=sg_w_s, sg_b_s=sg_b_s, g_out_na=g_out_na,
                  g_out_sg=g_out_sg, w_out=w_out, norm_mix_post=norm_mix_post,
                  norm_ffn_pre=norm_ffn_pre, w_ff1=w_ff1, w_ff2=w_ff2, norm_ffn_post=norm_ffn_post)
    x2 = x.reshape(b * s, d)
    for l in range(norm_mix_pre.shape[0]):
        x2 = _layer(x2, b, s, {k: v[l] for k, v in params.items()})
    return x2.reshape(b, s, d)
```
